```python
import math
import jax
import jax.numpy as jnp
from jax import lax
import numpy as np

D_MODEL = 1024
BATCH = 8
SEQ = 4096
DEPTH = 4

GRID_W = 64
CTX_LEN = 256
N_MOD = 6
N_BRANCH = 3
NORM_EPS = 1e-6
RNN_WIDTH = D_MODEL
RNN_BLOCKS = 16
RNN_BLOCK = RNN_WIDTH // RNN_BLOCKS
RNN_CONV = 4
RNN_PAD = (2, 1)
RNN_C = 8.0
ATTN_HEADS = 8
HEAD_DIM = 64
ATTN_WIDTH = ATTN_HEADS * 2 * HEAD_DIM
ROPE_PAIRS_AXIS = HEAD_DIM // 4
ROPE_BASE = 10000.0
Q_BLOCK = 128
CONV_WIDTH = D_MODEL
CONV_K = 31
CONV_PAD = ((CONV_K - 1) // 2, (CONV_K - 1) // 2)
D_FF = -(-8 * D_MODEL // (3 * 256)) * 256
OFF_RX = 0
OFF_K = OFF_RX + RNN_WIDTH
OFF_V = OFF_K + ATTN_WIDTH
CTX_COLS = OFF_V + ATTN_WIDTH
OFF_RG = CTX_COLS
OFF_Q = OFF_RG + RNN_WIDTH
OFF_CV = OFF_Q + ATTN_WIDTH
OFF_CG = OFF_CV + CONV_WIDTH
OFF_G = OFF_CG + CONV_WIDTH
IN_COLS = OFF_G + N_BRANCH * D_MODEL

kernel_name = 'hybrid_rglru_diffattn_conformer_dit'


def rmsnorm(x, g):
    xf = x.astype(jnp.float32)
    y = xf * lax.rsqrt(jnp.mean(xf * xf, axis=-1, keepdims=True) + NORM_EPS)
    return (y * g.astype(jnp.float32)).astype(x.dtype)


def layernorm(x, g, b):
    xf = x.astype(jnp.float32)
    mu = jnp.mean(xf, axis=-1, keepdims=True)
    var = jnp.mean(jnp.square(xf - mu), axis=-1, keepdims=True)
    y = (xf - mu) * lax.rsqrt(var + NORM_EPS)
    return (y * g.astype(jnp.float32) + b.astype(jnp.float32)).astype(x.dtype)


def modulate(h, shift, scale):
    return h * (1 + scale) + shift


def depthwise_conv(x, w, b, pad):
    y = lax.conv_general_dilated(x, w[:, None, :].astype(x.dtype), window_strides=(1,),
                                 padding=[pad], dimension_numbers=('NWC', 'WIO', 'NWC'),
                                 feature_group_count=x.shape[-1])
    return y + b.astype(x.dtype)


def blockdiag(x, w, b):
    B, L, _ = x.shape
    xr = x.reshape(B, L, RNN_BLOCKS, RNN_BLOCK)
    y = jnp.einsum('blni,nio->blno', xr, w.astype(jnp.float32))
    return y.reshape(B, L, RNN_WIDTH) + b.astype(jnp.float32)


def rglru_coeffs(xc, w_a, b_a, w_x, b_x, lam):
    xf = xc.astype(jnp.float32)
    r = jax.nn.sigmoid(blockdiag(xf, w_a, b_a))
    i = jax.nn.sigmoid(blockdiag(xf, w_x, b_x))
    log_a = -RNN_C * r * jax.nn.softplus(-lam.astype(jnp.float32))
    a = jnp.exp(log_a)
    mult = jnp.sqrt(-jnp.expm1(2.0 * log_a))
    return a, mult * (i * xf)


def _scan_combine(left, right):
    a1, b1 = left
    a2, b2 = right
    return a1 * a2, a2 * b1 + b2


def linear_recurrence(a, b, h0, reverse):
    if reverse:
        a, b = jnp.flip(a, 1), jnp.flip(b, 1)
    if h0 is not None:
        b = b.at[:, 0].add(a[:, 0] * h0)
    _, h = lax.associative_scan(_scan_combine, (a, b), axis=1)
    if reverse:
        h = jnp.flip(h, 1)
    return h


def rglru_bidir(xr_c, xr, w_a, b_a, w_x, b_x, lam, need_ctx_out):
    ys, ys_c = [], []
    for d in range(2):
        rev = d == 1
        a_c, b_c = rglru_coeffs(xr_c, w_a[d], b_a[d], w_x[d], b_x[d], lam[d])
        h_c = linear_recurrence(a_c, b_c, None, rev)
        h_end = h_c[:, 0] if rev else h_c[:, -1]
        a, bb = rglru_coeffs(xr, w_a[d], b_a[d], w_x[d], b_x[d], lam[d])
        ys.append(linear_recurrence(a, bb, h_end, rev))
        ys_c.append(h_c)
    y_c = ys_c[0] + ys_c[1] if need_ctx_out else None
    return y_c, ys[0] + ys[1]


def axial_rope(rows):
    r = jnp.repeat(jnp.arange(rows, dtype=jnp.float32), GRID_W)
    col = jnp.tile(jnp.arange(GRID_W, dtype=jnp.float32), rows)
    inv = ROPE_BASE ** (-jnp.arange(ROPE_PAIRS_AXIS, dtype=jnp.float32) / ROPE_PAIRS_AXIS)
    ang = jnp.concatenate([r[:, None] * inv, col[:, None] * inv], axis=-1)
    return jnp.cos(ang), jnp.sin(ang)


def apply_rope(t, cos, sin):
    half = HEAD_DIM // 2
    t1, t2 = t[..., :half], t[..., half:]
    cs = cos[None, :, None, None, :].astype(t.dtype)
    sn = sin[None, :, None, None, :].astype(t.dtype)
    return jnp.concatenate([t1 * cs - t2 * sn, t1 * sn + t2 * cs], axis=-1)


def diff_softmax_mix(q, k, v, lam):
    s = jnp.einsum('bqhmd,bkhmd->bhmqk', q, k).astype(jnp.float32) * (HEAD_DIM ** -0.5)
    p = jax.nn.softmax(s, axis=-1)
    w = p[:, :, 0] - lam.astype(jnp.float32) * p[:, :, 1]
    return jnp.einsum('bhqk,bkhe->bqhe', w, v.astype(jnp.float32))


def latent_diff_attention(q, k_all, v_all, lam):
    B, S = q.shape[0], q.shape[1]
    nb = S // Q_BLOCK
    qb = jnp.moveaxis(q.reshape(B, nb, Q_BLOCK, ATTN_HEADS, 2, HEAD_DIM), 1, 0)
    o = lax.map(lambda qq: diff_softmax_mix(qq, k_all, v_all, lam), qb)
    return jnp.moveaxis(o, 0, 1).reshape(B, S, ATTN_HEADS, 2 * HEAD_DIM)


def attn_post(o, g, lam_init, w_o, dtype):
    o = rmsnorm(o, g) * (1.0 - lam_init)
    return o.reshape(o.shape[0], o.shape[1], ATTN_WIDTH).astype(dtype) @ w_o


def conformer_conv(val, gate, dw_w, dw_b, ln_g, ln_b, w_o):
    z = val * jax.nn.sigmoid(gate)
    z = depthwise_conv(z, dw_w, dw_b, CONV_PAD)
    z = jax.nn.silu(layernorm(z, ln_g, ln_b))
    return z @ w_o


def gated_merge(gate_logits, y_r, y_a, y_c, w_o):
    g = jax.nn.sigmoid(gate_logits)
    m = (g[..., :D_MODEL] * y_r + g[..., D_MODEL:2 * D_MODEL] * y_a
         + g[..., 2 * D_MODEL:] * y_c)
    return m @ w_o


def swiglu(u, w_i, w_o):
    gu = u @ w_i
    return (jax.nn.silu(gu[..., :D_FF]) * gu[..., D_FF:]) @ w_o


def setup_inputs(seed: int = 0) -> dict:
    key = jax.random.key(seed)
    ks = jax.random.split(key, 30)

    def nrm(k, shape, scale):
        return scale * jax.random.normal(k, shape, jnp.float32)

    L = DEPTH
    u = jax.random.uniform(ks[16], (L, 2, RNN_WIDTH), jnp.float32, minval=0.9, maxval=0.999)
    a0 = u ** (1.0 / RNN_C)
    return {
        'x': nrm(ks[0], (BATCH, SEQ, D_MODEL), 1.0),
        'c': nrm(ks[1], (BATCH, D_MODEL), 1.0),
        'ctx': nrm(ks[2], (BATCH, CTX_LEN, D_MODEL), 1.0),
        'c_ctx': nrm(ks[3], (D_MODEL,), 1.0),
        'w_mod': nrm(ks[4], (L, D_MODEL, N_MOD * D_MODEL), 0.5 * D_MODEL ** -0.5),
        'b_mod': nrm(ks[5], (L, N_MOD * D_MODEL), 0.02),
        'g_norm1': 1.0 + nrm(ks[6], (L, D_MODEL), 0.02),
        'g_norm2': 1.0 + nrm(ks[7], (L, D_MODEL), 0.02),
        'w_in': nrm(ks[8], (L, D_MODEL, IN_COLS), D_MODEL ** -0.5),
        'b_in': nrm(ks[9], (L, IN_COLS), 0.02),
        'rnn_conv_w': nrm(ks[10], (L, RNN_CONV, RNN_WIDTH), RNN_CONV ** -0.5),
        'rnn_conv_b': nrm(ks[11], (L, RNN_WIDTH), 0.02),
        'rnn_w_a': nrm(ks[12], (L, 2, RNN_BLOCKS, RNN_BLOCK, RNN_BLOCK), RNN_BLOCK ** -0.5),
        'rnn_b_a': nrm(ks[13], (L, 2, RNN_WIDTH), 0.02),
        'rnn_w_x': nrm(ks[14], (L, 2, RNN_BLOCKS, RNN_BLOCK, RNN_BLOCK), RNN_BLOCK ** -0.5),
        'rnn_b_x': nrm(ks[15], (L, 2, RNN_WIDTH), 0.02),
        'rnn_lambda': jnp.log(a0) - jnp.log1p(-a0),
        'w_rnn_o': nrm(ks[17], (L, RNN_WIDTH, D_MODEL), RNN_WIDTH ** -0.5),
        'lambda_qk': nrm(ks[18], (L, 4, HEAD_DIM), 0.1),
        'g_subln': 1.0 + nrm(ks[19], (L, 2 * HEAD_DIM), 0.02),
        'w_attn_o': nrm(ks[20], (L, ATTN_WIDTH, D_MODEL), ATTN_WIDTH ** -0.5),
        'conv_dw_w': nrm(ks[21], (L, CONV_K, CONV_WIDTH), CONV_K ** -0.5),
        'conv_dw_b': nrm(ks[22], (L, CONV_WIDTH), 0.02),
        'conv_ln_g': 1.0 + nrm(ks[23], (L, CONV_WIDTH), 0.02),
        'conv_ln_b': nrm(ks[24], (L, CONV_WIDTH), 0.02),
        'w_conv_o': nrm(ks[25], (L, CONV_WIDTH, D_MODEL), CONV_WIDTH ** -0.5),
        'w_out': nrm(ks[26], (L, D_MODEL, D_MODEL), D_MODEL ** -0.5),
        'w_ffn_in': nrm(ks[27], (L, D_MODEL, 2 * D_FF), D_MODEL ** -0.5),
        'w_ffn_out': nrm(ks[28], (L, D_FF, D_MODEL), D_FF ** -0.5),
        'g_final': 1.0 + nrm(ks[29], (D_MODEL,), 0.02),
    }


def reference(x, c, ctx, c_ctx, w_mod, b_mod, g_norm1, g_norm2, w_in, b_in,
              rnn_conv_w, rnn_conv_b, rnn_w_a, rnn_b_a, rnn_w_x, rnn_b_x, rnn_lambda, w_rnn_o,
              lambda_qk, g_subln, w_attn_o,
              conv_dw_w, conv_dw_b, conv_ln_g, conv_ln_b, w_conv_o,
              w_out, w_ffn_in, w_ffn_out, g_final):
    dt = x.dtype
    B, S = x.shape[0], x.shape[1]
    CL = ctx.shape[1]
    rows = S // GRID_W
    cos, sin = axial_rope(rows)
    s_c = jax.nn.silu(c)
    s_cc = jax.nn.silu(c_ctx)
    h, hc = x, ctx
    for l in range(DEPTH):
        ctx_out = l < DEPTH - 1
        lam_init = 0.8 - 0.6 * math.exp(-0.3 * l)
        mod = (s_c @ w_mod[l] + b_mod[l])[:, None, :]
        mod_c = (s_cc @ w_mod[l] + b_mod[l])[None, None, :]
        sh1, sc1, ga1, sh2, sc2, ga2 = jnp.split(mod, N_MOD, axis=-1)
        csh1, csc1, cga1, csh2, csc2, cga2 = jnp.split(mod_c, N_MOD, axis=-1)

        u = modulate(rmsnorm(h, g_norm1[l]), sh1, sc1)
        uc = modulate(rmsnorm(hc, g_norm1[l]), csh1, csc1)
        p = u @ w_in[l] + b_in[l]
        ncol = IN_COLS if ctx_out else CTX_COLS
        pc = uc @ w_in[l][:, :ncol] + b_in[l][:ncol]

        xr = depthwise_conv(p[..., OFF_RX:OFF_K], rnn_conv_w[l], rnn_conv_b[l], RNN_PAD)
        xr_c = depthwise_conv(pc[..., OFF_RX:OFF_K], rnn_conv_w[l], rnn_conv_b[l], RNN_PAD)
        hr_c, hr = rglru_bidir(xr_c, xr, rnn_w_a[l], rnn_b_a[l], rnn_w_x[l], rnn_b_x[l],
                               rnn_lambda[l], ctx_out)
        y_r = (hr.astype(dt) * jax.nn.gelu(p[..., OFF_RG:OFF_Q], approximate=True)) @ w_rnn_o[l]

        q = apply_rope(p[..., OFF_Q:OFF_CV].reshape(B, S, ATTN_HEADS, 2, HEAD_DIM), cos, sin)
        k = apply_rope(p[..., OFF_K:OFF_V].reshape(B, S, ATTN_HEADS, 2, HEAD_DIM), cos, sin)
        v = p[..., OFF_V:CTX_COLS].reshape(B, S, ATTN_HEADS, 2 * HEAD_DIM)
        kc = pc[..., OFF_K:OFF_V].reshape(B, CL, ATTN_HEADS, 2, HEAD_DIM)
        vc = pc[..., OFF_V:CTX_COLS].reshape(B, CL, ATTN_HEADS, 2 * HEAD_DIM)
        lq = lambda_qk[l]
        lam = jnp.exp(jnp.sum(lq[0] * lq[1])) - jnp.exp(jnp.sum(lq[2] * lq[3])) + lam_init
        o = latent_diff_attention(q, jnp.concatenate([kc, k], axis=1),
                                  jnp.concatenate([vc, v], axis=1), lam)
        y_a = attn_post(o, g_subln[l], lam_init, w_attn_o[l], dt)

        y_c = conformer_conv(p[..., OFF_CV:OFF_CG], p[..., OFF_CG:OFF_G], conv_dw_w[l], conv_dw_b[l],
                             conv_ln_g[l], conv_ln_b[l], w_conv_o[l])

        h = h + ga1 * gated_merge(p[..., OFF_G:], y_r, y_a, y_c, w_out[l])
        h = h + ga2 * swiglu(modulate(rmsnorm(h, g_norm2[l]), sh2, sc2), w_ffn_in[l], w_ffn_out[l])

        if ctx_out:
            qc = pc[..., OFF_Q:OFF_CV].reshape(B, CL, ATTN_HEADS, 2, HEAD_DIM)
            yr_c = (hr_c.astype(dt) * jax.nn.gelu(pc[..., OFF_RG:OFF_Q], approximate=True)) @ w_rnn_o[l]
            ya_c = attn_post(diff_softmax_mix(qc, kc, vc, lam), g_subln[l], lam_init, w_attn_o[l], dt)
            yc_c = conformer_conv(pc[..., OFF_CV:OFF_CG], pc[..., OFF_CG:OFF_G], conv_dw_w[l],
                                  conv_dw_b[l], conv_ln_g[l], conv_ln_b[l], w_conv_o[l])
            hc = hc + cga1 * gated_merge(pc[..., OFF_G:], yr_c, ya_c, yc_c, w_out[l])
            hc = hc + cga2 * swiglu(modulate(rmsnorm(hc, g_norm2[l]), csh2, csc2),
                                    w_ffn_in[l], w_ffn_out[l])
    return rmsnorm(h, g_final)
```

```python
import functools
import math

import jax
import jax.numpy as jnp
from jax import lax
from jax.experimental import pallas as pl
from jax.experimental.pallas import tpu as pltpu

F32 = jnp.float32
BF16 = jnp.bfloat16

NORM_EPS = 1e-6
N_MOD = 6
ATTN_HEADS = 8
HEAD_DIM = 64
VALUE_DIM = 2 * HEAD_DIM
GRID_W = 64
ROPE_BASE = 10000.0
RNN_BLOCK = 64
RNN_CONV = 4
RNN_C = 8.0
CONV_K = 31
LANES = 128
SUBLANES = 8
BF16_ROWS = 16
HALO = 16
MXU_DIM = 256
VMEM_LIMIT = 56 * 1024 * 1024

COL_RX, COL_K, COL_V, COL_RG, COL_Q, COL_CV, COL_CG, COL_G = 0, 1, 2, 3, 4, 5, 6, 7
N_COL_BLOCKS = 10

ROW_TILE = 256
RNN_CHUNK = 256
ATTN_TQ = 256
ATTN_TK = 256


def _sigmoid(v):
    return 1.0 / (1.0 + jnp.exp(-v))


def _silu(v):
    return v * _sigmoid(v)


def _gelu_tanh(v):
    return 0.5 * v * (1.0 + jnp.tanh(math.sqrt(2.0 / math.pi) * (v + 0.044715 * (v * v * v))))


def _params(*sem):
    return pltpu.CompilerParams(dimension_semantics=sem, vmem_limit_bytes=VMEM_LIMIT)


def _mod_kernel(c_ref, w_ref, b_ref, o_ref):
    s = _silu(c_ref[...])
    o_ref[...] = jnp.dot(s, w_ref[...], preferred_element_type=F32,
                         precision=lax.Precision.HIGHEST) + b_ref[...]


def _modulation(cc, w_mod, b_mod):
    L, D, _ = w_mod.shape
    R = cc.shape[0]
    return pl.pallas_call(
        _mod_kernel,
        grid=(L, N_MOD),
        in_specs=[pl.BlockSpec((R, D), lambda l, j: (0, 0)),
                  pl.BlockSpec((None, D, D), lambda l, j: (l, 0, j)),
                  pl.BlockSpec((None, 1, D), lambda l, j: (l, 0, j))],
        out_specs=pl.BlockSpec((None, R, D), lambda l, j: (l, 0, j)),
        out_shape=jax.ShapeDtypeStruct((L, R, N_MOD * D), F32),
        compiler_params=_params("arbitrary", "arbitrary"),
        name="modulation",
    )(cc, w_mod, b_mod.reshape(L, 1, N_MOD * D))


def _seg_index(n_ctx_tiles):
    return lambda i: jnp.where(i >= n_ctx_tiles, 1, 0)


def _mod_norm(x, g, shift, scale):
    y = x * lax.rsqrt(jnp.mean(x * x, axis=-1, keepdims=True) + NORM_EPS) * g
    return y * (1.0 + scale) + shift


def _modnorm_kernel(h_ref, g_ref, m_ref, o_ref):
    m = m_ref[...]
    o_ref[...] = _mod_norm(h_ref[...], g_ref[...], m[0:1], m[1:2]).astype(o_ref.dtype)


def _modnorm(h, g, modtab, cl):
    B, T, D = h.shape
    seg = _seg_index(cl // ROW_TILE)
    return pl.pallas_call(
        _modnorm_kernel,
        grid=(B, T // ROW_TILE),
        in_specs=[pl.BlockSpec((None, ROW_TILE, D), lambda b, i: (b, i, 0)),
                  pl.BlockSpec((1, D), lambda b, i: (0, 0)),
                  pl.BlockSpec((None, None, 8, D), lambda b, i: (b, seg(i), 0, 0))],
        out_specs=pl.BlockSpec((None, ROW_TILE, D), lambda b, i: (b, i, 0)),
        out_shape=jax.ShapeDtypeStruct((B, T, D), BF16),
        compiler_params=_params("parallel", "parallel"),
        name="modnorm",
    )(h, g.reshape(1, D), modtab)


def _rope(t, cos, sin_signed):
    rows = t.shape[0]
    lane = lax.broadcasted_iota(jnp.int32, (rows, LANES), 1)
    first_half = (lane % HEAD_DIM) < (HEAD_DIM // 2)
    outs = []
    for k in range(t.shape[1] // LANES):
        tk = t[:, k * LANES:(k + 1) * LANES]
        partner = jnp.where(first_half, pltpu.roll(tk, LANES - HEAD_DIM // 2, 1),
                            pltpu.roll(tk, HEAD_DIM // 2, 1))
        outs.append(tk * cos + partner * sin_signed)
    return jnp.concatenate(outs, axis=1)


def _proj_kernel(u_ref, w_ref, b_ref, cos_ref, sin_ref, o_ref):
    j = pl.program_id(0)
    acc = jnp.dot(u_ref[...], w_ref[...], preferred_element_type=F32) + b_ref[...]
    is_rope = jnp.logical_or(j == COL_K, j == COL_Q)

    @pl.when(is_rope)
    def _():
        o_ref[...] = _rope(acc, cos_ref[...], sin_ref[...]).astype(o_ref.dtype)

    @pl.when(jnp.logical_not(is_rope))
    def _():
        o_ref[...] = acc.astype(o_ref.dtype)


def _in_proj(u2, w, b, cos_t, sin_t, tm):
    M, D = u2.shape
    N = w.shape[1]
    T = cos_t.shape[0]
    tiles_per_seq = T // tm
    return pl.pallas_call(
        _proj_kernel,
        grid=(N // D, M // tm),
        in_specs=[pl.BlockSpec((tm, D), lambda j, i: (i, 0)),
                  pl.BlockSpec((D, D), lambda j, i: (0, j)),
                  pl.BlockSpec((1, D), lambda j, i: (0, j)),
                  pl.BlockSpec((tm, LANES), lambda j, i: (i % tiles_per_seq, 0)),
                  pl.BlockSpec((tm, LANES), lambda j, i: (i % tiles_per_seq, 0))],
        out_specs=pl.BlockSpec((tm, D), lambda j, i: (i, j)),
        out_shape=jax.ShapeDtypeStruct((M, N), BF16),
        compiler_params=_params("parallel", "parallel"),
        name="in_proj",
    )(u2, w, b.reshape(1, N), cos_t, sin_t)


def _conv_rows(ext, taps, rows_out):
    n = ext.shape[0]
    rolled = {}
    acc = None
    for off, w in taps:
        start = HALO + off
        r, q = start % SUBLANES, start // SUBLANES
        if r not in rolled:
            rolled[r] = ext if r == 0 else pltpu.roll(ext, n - r, 0)
        term = w * rolled[r][q * SUBLANES:q * SUBLANES + rows_out]
        acc = term if acc is None else acc + term
    return acc


def _rglru_kernel(x_ref, cw_ref, cb_ref, wa_ref, ba_ref, wx_ref, bx_ref, lam_ref, o_ref,
                  a_scr, b_scr, *, T, CL, TC):
    d = pl.program_id(2)
    fwd = d == 0
    n_chunks, n_ctx = T // TC, CL // TC
    G = TC // SUBLANES
    cb = x_ref.shape[1]
    cw = cw_ref[...]
    taps = [(k - 2, cw[k:k + 1]) for k in range(RNN_CONV)]
    neg_lam = -lam_ref[...]
    softplus = jnp.maximum(neg_lam, 0.0) + jnp.log(1.0 + jnp.exp(-jnp.abs(neg_lam)))
    sub = lax.broadcasted_iota(jnp.int32, (G, SUBLANES, cb), 1)

    def chunk(step, carry):
        c_bwd = jnp.where(step < n_ctx, n_ctx - 1 - step, n_chunks - 1 - (step - n_ctx))
        ci = jnp.where(fwd, step, c_bwd)
        t0 = pl.multiple_of(ci * TC, TC)
        seg_first = jnp.logical_or(ci == 0, ci == n_ctx)
        seg_last = jnp.logical_or(ci == n_ctx - 1, ci == n_chunks - 1)
        p0 = pl.multiple_of(jnp.maximum(t0 - HALO, 0), HALO)
        n0 = pl.multiple_of(jnp.minimum(t0 + TC, T - HALO), HALO)
        prev = x_ref[pl.ds(p0, HALO), :].astype(F32) * jnp.where(seg_first, 0.0, 1.0)
        nxt = x_ref[pl.ds(n0, HALO), :].astype(F32) * jnp.where(seg_last, 0.0, 1.0)
        main = x_ref[pl.ds(t0, TC), :].astype(F32)
        xc = _conv_rows(jnp.concatenate([prev, main, nxt], axis=0), taps, TC) + cb_ref[...]
        xb = xc.astype(BF16)
        r = _sigmoid(jnp.dot(xb, wa_ref[...], preferred_element_type=F32) + ba_ref[...])
        gi = _sigmoid(jnp.dot(xb, wx_ref[...], preferred_element_type=F32) + bx_ref[...])
        log_a = (-RNN_C) * r * softplus
        a = jnp.exp(log_a)
        bb = jnp.sqrt(1.0 - a * a) * (gi * xc)
        a3 = a.reshape(G, SUBLANES, cb)
        b3 = bb.reshape(G, SUBLANES, cb)
        for s in (1, 2, 4):
            a_f, b_f = pltpu.roll(a3, s, 1), pltpu.roll(b3, s, 1)
            a_b, b_b = pltpu.roll(a3, SUBLANES - s, 1), pltpu.roll(b3, SUBLANES - s, 1)
            use = jnp.logical_and(sub >= jnp.where(fwd, s, 0), sub < jnp.where(fwd, SUBLANES, SUBLANES - s))
            a_n = jnp.where(fwd, a_f, a_b)
            b_n = jnp.where(fwd, b_f, b_b)
            b3 = jnp.where(use, a3 * b_n + b3, b3)
            a3 = jnp.where(use, a3 * a_n, a3)
        a_scr[...] = a3
        b_scr[...] = b3

        def group(k, hc):
            g = jnp.where(fwd, k, G - 1 - k)
            hg = b_scr[g] + a_scr[g] * hc
            b_scr[g] = hg
            return jnp.where(fwd, hg[SUBLANES - 1:SUBLANES], hg[0:1])

        carry = lax.fori_loop(0, G, group, carry)
        o_ref[pl.ds(t0, TC), :] = b_scr[...].reshape(TC, cb).astype(o_ref.dtype)
        return carry

    lax.fori_loop(0, n_chunks, chunk, jnp.zeros((1, cb), F32))


def _rglru(p3, conv_w, conv_b, wa, ba, wx, bx, lam, cl):
    B, T, _ = p3.shape
    D = conv_w.shape[1]
    nb = D // MXU_DIM
    vec = lambda: pl.BlockSpec((None, 1, MXU_DIM), lambda b, c, d: (d, 0, c))
    mat = lambda: pl.BlockSpec((None, None, MXU_DIM, MXU_DIM), lambda b, c, d: (d, c, 0, 0))
    G = RNN_CHUNK // SUBLANES
    return pl.pallas_call(
        functools.partial(_rglru_kernel, T=T, CL=cl, TC=RNN_CHUNK),
        grid=(B, nb, 2),
        in_specs=[pl.BlockSpec((None, T, MXU_DIM), lambda b, c, d: (b, 0, COL_RX * nb + c)),
                  pl.BlockSpec((RNN_CONV, MXU_DIM), lambda b, c, d: (0, c)),
                  pl.BlockSpec((1, MXU_DIM), lambda b, c, d: (0, c)),
                  mat(), vec(), mat(), vec(), vec()],
        out_specs=pl.BlockSpec((None, None, T, MXU_DIM), lambda b, c, d: (d, b, 0, c)),
        out_shape=jax.ShapeDtypeStruct((2, B, T, D), BF16),
        scratch_shapes=[pltpu.VMEM((G, SUBLANES, MXU_DIM), F32),
                        pltpu.VMEM((G, SUBLANES, MXU_DIM), F32)],
        compiler_params=_params("parallel", "parallel", "arbitrary"),
        name="rglru",
    )(p3, conv_w, conv_b.reshape(1, D), wa, ba.reshape(2, 1, D), wx, bx.reshape(2, 1, D),
      lam.reshape(2, 1, D))


def _attn_kernel(q_ref, k_ref, v_ref, lq_ref, g_ref, o_ref, *, T, CL, TK, lam_init):
    iq = pl.program_id(2)
    tq = q_ref.shape[0]
    q = q_ref[...].astype(F32) * (HEAD_DIM ** -0.5)
    lane = lax.broadcasted_iota(jnp.int32, q.shape, 1)
    q0 = jnp.where(lane < HEAD_DIM, q, 0.0).astype(BF16)
    q1 = jnp.where(lane >= HEAD_DIM, q, 0.0).astype(BF16)
    n_k = jnp.where(iq * tq < CL, CL // TK, T // TK)

    def update(s, vt, m, l, acc):
        m_new = jnp.maximum(m, jnp.max(s, axis=1, keepdims=True))
        alpha = jnp.exp(m - m_new)
        p = jnp.exp(s - m_new)
        l = alpha * l + jnp.sum(p, axis=1, keepdims=True)
        acc = alpha * acc + jnp.dot(p.astype(BF16), vt, preferred_element_type=F32)
        return m_new, l, acc

    def body(j, carry):
        m0, l0, acc0, m1, l1, acc1 = carry
        k0 = pl.multiple_of(j * TK, TK)
        kt = k_ref[pl.ds(k0, TK), :]
        vt = v_ref[pl.ds(k0, TK), :]
        dims = (((1,), (1,)), ((), ()))
        s0 = lax.dot_general(q0, kt, dims, preferred_element_type=F32)
        s1 = lax.dot_general(q1, kt, dims, preferred_element_type=F32)
        m0, l0, acc0 = update(s0, vt, m0, l0, acc0)
        m1, l1, acc1 = update(s1, vt, m1, l1, acc1)
        return m0, l0, acc0, m1, l1, acc1

    col = lambda v: jnp.full((tq, 1), v, F32)
    mat = jnp.zeros((tq, VALUE_DIM), F32)
    _, l0, acc0, _, l1, acc1 = lax.fori_loop(
        0, n_k, body, (col(-1e30), col(0.0), mat, col(-1e30), col(0.0), mat))
    lq = lq_ref[...]
    lam = (jnp.exp(jnp.sum(lq[0:1] * lq[1:2], axis=1, keepdims=True))
           - jnp.exp(jnp.sum(lq[2:3] * lq[3:4], axis=1, keepdims=True)) + lam_init)
    o = acc0 / l0 - lam * (acc1 / l1)
    y = o * lax.rsqrt(jnp.mean(o * o, axis=1, keepdims=True) + NORM_EPS) * g_ref[...]
    o_ref[...] = (y * (1.0 - lam_init)).astype(o_ref.dtype)


def _attention(p3, lq, g_subln, cl, lam_init):
    B, T, _ = p3.shape
    D = ATTN_HEADS * VALUE_DIM
    cpb = D // VALUE_DIM
    return pl.pallas_call(
        functools.partial(_attn_kernel, T=T, CL=cl, TK=ATTN_TK, lam_init=lam_init),
        grid=(B, ATTN_HEADS, T // ATTN_TQ),
        in_specs=[pl.BlockSpec((None, ATTN_TQ, VALUE_DIM), lambda b, h, i: (b, i, COL_Q * cpb + h)),
                  pl.BlockSpec((None, T, VALUE_DIM), lambda b, h, i: (b, 0, COL_K * cpb + h)),
                  pl.BlockSpec((None, T, VALUE_DIM), lambda b, h, i: (b, 0, COL_V * cpb + h)),
                  pl.BlockSpec((4, HEAD_DIM), lambda b, h, i: (0, 0)),
                  pl.BlockSpec((1, VALUE_DIM), lambda b, h, i: (0, 0))],
        out_specs=pl.BlockSpec((None, ATTN_TQ, VALUE_DIM), lambda b, h, i: (b, i, h)),
        out_shape=jax.ShapeDtypeStruct((B, T, D), BF16),
        compiler_params=_params("parallel", "parallel", "arbitrary"),
        name="diff_attention",
    )(p3, p3, p3, lq, g_subln.reshape(1, VALUE_DIM))


def _conv_kernel(v_ref, vp_ref, vn_ref, g_ref, gp_ref, gn_ref, w_ref, b_ref, lg_ref, lb_ref, o_ref,
                 *, T, CL, TC):
    i = pl.program_id(1)
    n_ctx, n_chunks = CL // TC, T // TC
    seg_first = jnp.logical_or(i == 0, i == n_ctx)
    seg_last = jnp.logical_or(i == n_ctx - 1, i == n_chunks - 1)

    def gated(v, g):
        return v[...].astype(F32) * _sigmoid(g[...].astype(F32))

    ext = jnp.concatenate([gated(vp_ref, gp_ref) * jnp.where(seg_first, 0.0, 1.0),
                           gated(v_ref, g_ref),
                           gated(vn_ref, gn_ref) * jnp.where(seg_last, 0.0, 1.0)], axis=0)
    w = w_ref[...]
    cols = []
    for c in range(ext.shape[1] // LANES):
        sl = slice(c * LANES, (c + 1) * LANES)
        taps = [(k - (CONV_K - 1) // 2, w[k:k + 1, sl]) for k in range(CONV_K)]
        cols.append(_conv_rows(ext[:, sl], taps, TC))
    z = jnp.concatenate(cols, axis=1) + b_ref[...]
    mu = jnp.mean(z, axis=-1, keepdims=True)
    zc = z - mu
    var = jnp.mean(zc * zc, axis=-1, keepdims=True)
    y = zc * lax.rsqrt(var + NORM_EPS) * lg_ref[...] + lb_ref[...]
    o_ref[...] = _silu(y).astype(o_ref.dtype)


def _conformer_conv(p3, dw_w, dw_b, ln_g, ln_b, cl):
    B, T, _ = p3.shape
    D = dw_w.shape[1]
    TC = ROW_TILE
    hb = TC // HALO
    last = T // HALO - 1

    def main(col):
        return pl.BlockSpec((None, TC, D), lambda b, i: (b, i, col))

    def prev(col):
        return pl.BlockSpec((None, HALO, D), lambda b, i: (b, jnp.maximum(i * hb - 1, 0), col))

    def nxt(col):
        return pl.BlockSpec((None, HALO, D), lambda b, i: (b, jnp.minimum((i + 1) * hb, last), col))

    vec = lambda: pl.BlockSpec((1, D), lambda b, i: (0, 0))
    return pl.pallas_call(
        functools.partial(_conv_kernel, T=T, CL=cl, TC=TC),
        grid=(B, T // TC),
        in_specs=[main(COL_CV), prev(COL_CV), nxt(COL_CV), main(COL_CG), prev(COL_CG), nxt(COL_CG),
                  pl.BlockSpec((CONV_K, D), lambda b, i: (0, 0)), vec(), vec(), vec()],
        out_specs=pl.BlockSpec((None, TC, D), lambda b, i: (b, i, 0)),
        out_shape=jax.ShapeDtypeStruct((B, T, D), BF16),
        compiler_params=_params("parallel", "parallel"),
        name="conformer_conv",
    )(p3, p3, p3, p3, p3, p3, dw_w, dw_b.reshape(1, D), ln_g.reshape(1, D), ln_b.reshape(1, D))


def _merge_kernel(hf_ref, hb_ref, rg_ref, oa_ref, zc_ref, g0_ref, g1_ref, g2_ref, h_ref, m_ref,
                  wr_ref, wa_ref, wc_ref, wo_ref, o_ref):
    f32 = lambda r: r[...].astype(F32)
    rec = (f32(hf_ref) + f32(hb_ref)) * _gelu_tanh(f32(rg_ref))
    y_r = jnp.dot(rec.astype(BF16), wr_ref[...], preferred_element_type=F32)
    y_a = jnp.dot(oa_ref[...], wa_ref[...], preferred_element_type=F32)
    y_c = jnp.dot(zc_ref[...], wc_ref[...], preferred_element_type=F32)
    mix = _sigmoid(f32(g0_ref)) * y_r + _sigmoid(f32(g1_ref)) * y_a + _sigmoid(f32(g2_ref)) * y_c
    upd = jnp.dot(mix.astype(BF16), wo_ref[...], preferred_element_type=F32)
    o_ref[...] = h_ref[...] + m_ref[...][2:3] * upd


def _merge(hdir, p3, oa, zc, h, modtab, w_r, w_a, w_c, w_o, cl):
    B, T, D = h.shape
    seg = _seg_index(cl // ROW_TILE)
    tile = lambda col: pl.BlockSpec((None, ROW_TILE, D), lambda b, i: (b, i, col))
    wspec = lambda: pl.BlockSpec((D, D), lambda b, i: (0, 0))
    return pl.pallas_call(
        _merge_kernel,
        grid=(B, T // ROW_TILE),
        in_specs=[pl.BlockSpec((None, None, ROW_TILE, D), lambda b, i: (0, b, i, 0)),
                  pl.BlockSpec((None, None, ROW_TILE, D), lambda b, i: (1, b, i, 0)),
                  tile(COL_RG), tile(0), tile(0), tile(COL_G), tile(COL_G + 1), tile(COL_G + 2),
                  tile(0),
                  pl.BlockSpec((None, None, 8, D), lambda b, i: (b, seg(i), 0, 0)),
                  wspec(), wspec(), wspec(), wspec()],
        out_specs=tile(0),
        out_shape=jax.ShapeDtypeStruct((B, T, D), F32),
        input_output_aliases={8: 0},
        compiler_params=_params("parallel", "parallel"),
        name="merge",
    )(hdir, hdir, p3, oa, zc, p3, p3, p3, h, modtab, w_r, w_a, w_c, w_o)


def _ffn_kernel(h_ref, g_ref, m_ref, wi_ref, wo_ref, o_ref, *, d_ff):
    h = h_ref[...]
    m = m_ref[...]
    u = _mod_norm(h, g_ref[...], m[3:4], m[4:5]).astype(BF16)
    gu = jnp.dot(u, wi_ref[...], preferred_element_type=F32)
    act = (_silu(gu[:, :d_ff]) * gu[:, d_ff:]).astype(BF16)
    o_ref[...] = h + m[5:6] * jnp.dot(act, wo_ref[...], preferred_element_type=F32)


def _ffn(h, g, modtab, w_i, w_o, cl):
    B, T, D = h.shape
    d_ff = w_o.shape[0]
    seg = _seg_index(cl // ROW_TILE)
    tile = pl.BlockSpec((None, ROW_TILE, D), lambda b, i: (b, i, 0))
    return pl.pallas_call(
        functools.partial(_ffn_kernel, d_ff=d_ff),
        grid=(B, T // ROW_TILE),
        in_specs=[tile,
                  pl.BlockSpec((1, D), lambda b, i: (0, 0)),
                  pl.BlockSpec((None, None, 8, D), lambda b, i: (b, seg(i), 0, 0)),
                  pl.BlockSpec((D, 2 * d_ff), lambda b, i: (0, 0)),
                  pl.BlockSpec((d_ff, D), lambda b, i: (0, 0))],
        out_specs=tile,
        out_shape=jax.ShapeDtypeStruct((B, T, D), F32),
        input_output_aliases={0: 0},
        compiler_params=_params("parallel", "parallel"),
        name="ffn",
    )(h, g.reshape(1, D), modtab, w_i, w_o)


def _final_kernel(h_ref, g_ref, o_ref):
    x = h_ref[...]
    o_ref[...] = x * lax.rsqrt(jnp.mean(x * x, axis=-1, keepdims=True) + NORM_EPS) * g_ref[...]


def _final_norm(h, g, cl):
    B, T, D = h.shape
    S = T - cl
    skip = cl // ROW_TILE
    return pl.pallas_call(
        _final_kernel,
        grid=(B, S // ROW_TILE),
        in_specs=[pl.BlockSpec((None, ROW_TILE, D), lambda b, i: (b, i + skip, 0)),
                  pl.BlockSpec((1, D), lambda b, i: (0, 0))],
        out_specs=pl.BlockSpec((None, ROW_TILE, D), lambda b, i: (b, i, 0)),
        out_shape=jax.ShapeDtypeStruct((B, S, D), F32),
        compiler_params=_params("parallel", "parallel"),
        name="final_norm",
    )(h, g.reshape(1, D))


def _rope_tables(cl, s):
    pairs_axis = HEAD_DIM // 4
    rows = jnp.repeat(jnp.arange(s // GRID_W, dtype=F32), GRID_W)
    cols = jnp.tile(jnp.arange(GRID_W, dtype=F32), s // GRID_W)
    inv = ROPE_BASE ** (-jnp.arange(pairs_axis, dtype=F32) / pairs_axis)
    ang = jnp.concatenate([rows[:, None] * inv, cols[:, None] * inv], axis=-1)
    cos = jnp.concatenate([jnp.ones((cl, HEAD_DIM // 2), F32), jnp.cos(ang)], axis=0)
    sin = jnp.concatenate([jnp.zeros((cl, HEAD_DIM // 2), F32), jnp.sin(ang)], axis=0)
    reps = LANES // HEAD_DIM
    return (jnp.tile(jnp.concatenate([cos, cos], axis=1), (1, reps)),
            jnp.tile(jnp.concatenate([-sin, sin], axis=1), (1, reps)))


def _blockdiag_tiles(w):
    two, nb, bs, _ = w.shape
    per = MXU_DIM // bs
    w = w.reshape(two, nb // per, per, bs, bs)
    eye = jnp.eye(per, dtype=w.dtype)
    t = jnp.einsum('dtpio,pq->dtpiqo', w, eye)
    return t.reshape(two, nb // per, MXU_DIM, MXU_DIM)


def kernel(x, c, ctx, c_ctx, w_mod, b_mod, g_norm1, g_norm2, w_in, b_in, rnn_conv_w, rnn_conv_b, rnn_w_a, rnn_b_a, rnn_w_x, rnn_b_x, rnn_lambda, w_rnn_o, lambda_qk, g_subln, w_attn_o, conv_dw_w, conv_dw_b, conv_ln_g, conv_ln_b, w_conv_o, w_out, w_ffn_in, w_ffn_out, g_final):
    B, S, D = x.shape
    CL = ctx.shape[1]
    T = CL + S
    L = w_mod.shape[0]
    assert D == ATTN_HEADS * VALUE_DIM and S % GRID_W == 0
    assert CL % ROW_TILE == 0 and S % ROW_TILE == 0 and T % 4 == 0 and (T // 4) % BF16_ROWS == 0

    h = jnp.concatenate([ctx, x], axis=1)

    n_cond = -(-(B + 1) // SUBLANES) * SUBLANES
    cc = jnp.zeros((n_cond, D), F32).at[:B].set(c).at[B].set(c_ctx)
    mod = _modulation(cc, w_mod, b_mod).reshape(L, n_cond, N_MOD, D)
    mod_ctx = jnp.broadcast_to(mod[:, B][:, None], (L, B, N_MOD, D))
    modtab = jnp.stack([mod_ctx, mod[:, :B]], axis=2)
    modtab = jnp.pad(modtab, ((0, 0), (0, 0), (0, 0), (0, 8 - N_MOD), (0, 0)))

    cos_t, sin_t = _rope_tables(CL, S)

    for l in range(L):
        lam_init = 0.8 - 0.6 * math.exp(-0.3 * l)
        u = _modnorm(h, g_norm1[l], modtab[l], CL)
        p = _in_proj(u.reshape(B * T, D), w_in[l].astype(BF16), b_in[l], cos_t, sin_t, T // 4)
        p3 = p.reshape(B, T, N_COL_BLOCKS * D)
        hdir = _rglru(p3, rnn_conv_w[l], rnn_conv_b[l],
                      _blockdiag_tiles(rnn_w_a[l]).astype(BF16), rnn_b_a[l],
                      _blockdiag_tiles(rnn_w_x[l]).astype(BF16), rnn_b_x[l], rnn_lambda[l], CL)
        oa = _attention(p3, lambda_qk[l], g_subln[l], CL, lam_init)
        zc = _conformer_conv(p3, conv_dw_w[l], conv_dw_b[l], conv_ln_g[l], conv_ln_b[l], CL)
        h = _merge(hdir, p3, oa, zc, h, modtab[l], w_rnn_o[l].astype(BF16), w_attn_o[l].astype(BF16),
                   w_conv_o[l].astype(BF16), w_out[l].astype(BF16), CL)
        h = _ffn(h, g_norm2[l], modtab[l], w_ffn_in[l].astype(BF16), w_ffn_out[l].astype(BF16), CL)
    return _final_norm(h, g_final, CL)
```

```python
import functools
import math

import jax
import jax.numpy as jnp
from jax import lax
from jax.experimental import pallas as pl
from jax.experimental.pallas import tpu as pltpu

F32 = jnp.float32
BF16 = jnp.bfloat16

NORM_EPS = 1e-6
N_MOD = 6
ATTN_HEADS = 8
HEAD_DIM = 64
VALUE_DIM = 2 * HEAD_DIM
GRID_W = 64
ROPE_BASE = 10000.0
RNN_BLOCK = 64
RNN_CONV = 4
RNN_C = 8.0
CONV_K = 31
LANES = 128
SUBLANES = 8
BF16_ROWS = 16
HALO = 16
MXU_DIM = 256
VMEM_LIMIT = 56 * 1024 * 1024

COL_RX, COL_K, COL_V, COL_RG, COL_Q, COL_CV, COL_CG, COL_G = 0, 1, 2, 3, 4, 5, 6, 7
N_COL_BLOCKS = 10

ROW_TILE = 256
RNN_CHUNK = 256
ATTN_TQ = 256
ATTN_KC = 256


def _sigmoid(v):
    return 1.0 / (1.0 + jnp.exp(-v))


def _silu(v):
    return v * _sigmoid(v)


def _gelu_tanh(v):
    return 0.5 * v * (1.0 + jnp.tanh(math.sqrt(2.0 / math.pi) * (v + 0.044715 * (v * v * v))))


def _params(*sem):
    return pltpu.CompilerParams(dimension_semantics=sem, vmem_limit_bytes=VMEM_LIMIT)


def _mod_kernel(c_ref, w_ref, b_ref, o_ref):
    s = _silu(c_ref[...])
    o_ref[...] = jnp.dot(s, w_ref[...], preferred_element_type=F32,
                         precision=lax.Precision.HIGHEST) + b_ref[...]


def _modulation(cc, w_mod, b_mod):
    L, D, _ = w_mod.shape
    R = cc.shape[0]
    return pl.pallas_call(
        _mod_kernel,
        grid=(L, N_MOD),
        in_specs=[pl.BlockSpec((R, D), lambda l, j: (0, 0)),
                  pl.BlockSpec((None, D, D), lambda l, j: (l, 0, j)),
                  pl.BlockSpec((None, 1, D), lambda l, j: (l, 0, j))],
        out_specs=pl.BlockSpec((None, R, D), lambda l, j: (l, 0, j)),
        out_shape=jax.ShapeDtypeStruct((L, R, N_MOD * D), F32),
        compiler_params=_params("arbitrary", "arbitrary"),
        name="modulation",
    )(cc, w_mod, b_mod.reshape(L, 1, N_MOD * D))


def _seg_index(n_ctx_tiles):
    return lambda i: jnp.where(i >= n_ctx_tiles, 1, 0)


def _mod_norm(x, g, shift, scale):
    y = x * lax.rsqrt(jnp.mean(x * x, axis=-1, keepdims=True) + NORM_EPS) * g
    return y * (1.0 + scale) + shift


def _rope(t, cos, sin_signed):
    outs = []
    for k in range(t.shape[1] // LANES):
        tk = t[:, k * LANES:(k + 1) * LANES]
        outs.append(tk * cos + pltpu.roll(tk, LANES // 2, 1) * sin_signed)
    return jnp.concatenate(outs, axis=1)


def _proj_kernel(h_ref, g_ref, m_ref, w_ref, b_ref, cos_ref, sin_ref, o_ref, u_scr, *, CL, tiles_per_seq):
    i, j = pl.program_id(0), pl.program_id(1)
    tm = h_ref.shape[0]

    @pl.when(j == 0)
    def _():
        m = m_ref[...]
        row = (i % tiles_per_seq) * tm + lax.broadcasted_iota(jnp.int32, (tm, 1), 0)
        is_ctx = row < CL
        shift = jnp.where(is_ctx, m[0, 0:1], m[1, 0:1])
        scale = jnp.where(is_ctx, m[0, 1:2], m[1, 1:2])
        u_scr[...] = _mod_norm(h_ref[...], g_ref[...], shift, scale).astype(u_scr.dtype)

    acc = jnp.dot(u_scr[...], w_ref[...], preferred_element_type=F32) + b_ref[...]
    is_rope = jnp.logical_or(j == COL_K, j == COL_Q)

    @pl.when(is_rope)
    def _():
        o_ref[...] = _rope(acc, cos_ref[...], sin_ref[...]).astype(o_ref.dtype)

    @pl.when(jnp.logical_not(is_rope))
    def _():
        o_ref[...] = acc.astype(o_ref.dtype)


def _in_proj(h2, g, modtab, w, b, cos_t, sin_t, cl, tm):
    M, D = h2.shape
    N = w.shape[1]
    T = cos_t.shape[0]
    tiles_per_seq = T // tm
    return pl.pallas_call(
        functools.partial(_proj_kernel, CL=cl, tiles_per_seq=tiles_per_seq),
        grid=(M // tm, N // D),
        in_specs=[pl.BlockSpec((tm, D), lambda i, j: (i, 0)),
                  pl.BlockSpec((1, D), lambda i, j: (0, 0)),
                  pl.BlockSpec((None, 2, 8, D), lambda i, j: (i // tiles_per_seq, 0, 0, 0)),
                  pl.BlockSpec((D, D), lambda i, j: (0, j)),
                  pl.BlockSpec((1, D), lambda i, j: (0, j)),
                  pl.BlockSpec((tm, LANES), lambda i, j: (i % tiles_per_seq, 0)),
                  pl.BlockSpec((tm, LANES), lambda i, j: (i % tiles_per_seq, 0))],
        out_specs=pl.BlockSpec((tm, D), lambda i, j: (i, j)),
        out_shape=jax.ShapeDtypeStruct((M, N), BF16),
        scratch_shapes=[pltpu.VMEM((tm, D), BF16)],
        compiler_params=_params("parallel", "arbitrary"),
        name="in_proj",
    )(h2, g.reshape(1, D), modtab, w, b.reshape(1, N), cos_t, sin_t)


def _conv_rows(ext, taps, rows_out):
    n = ext.shape[0]
    rolled = {}
    acc = None
    for off, w in taps:
        start = HALO + off
        r, q = start % SUBLANES, start // SUBLANES
        if r not in rolled:
            rolled[r] = ext if r == 0 else pltpu.roll(ext, n - r, 0)
        term = w * rolled[r][q * SUBLANES:q * SUBLANES + rows_out]
        acc = term if acc is None else acc + term
    return acc


def _rglru_kernel(x_ref, cw_ref, cb_ref, wa_ref, ba_ref, wx_ref, bx_ref, lam_ref, o_ref,
                  a_scr, b_scr, *, T, CL, TC):
    n_chunks, n_ctx = T // TC, CL // TC
    G = TC // SUBLANES
    cb = x_ref.shape[1]
    cw = cw_ref[...]
    taps = [(k - 2, cw[k:k + 1]) for k in range(RNN_CONV)]
    neg_lam = -lam_ref[...]
    softplus = jnp.maximum(neg_lam, 0.0) + jnp.log(1.0 + jnp.exp(-jnp.abs(neg_lam)))
    sub = lax.broadcasted_iota(jnp.int32, (G, SUBLANES, cb), 1)

    def run(fwd):
        def chunk(step, carry):
            if fwd:
                ci = step
            else:
                ci = jnp.where(step < n_ctx, n_ctx - 1 - step, n_chunks - 1 - (step - n_ctx))
            t0 = pl.multiple_of(ci * TC, TC)
            seg_first = jnp.logical_or(ci == 0, ci == n_ctx)
            seg_last = jnp.logical_or(ci == n_ctx - 1, ci == n_chunks - 1)
            p0 = pl.multiple_of(jnp.maximum(t0 - HALO, 0), HALO)
            n0 = pl.multiple_of(jnp.minimum(t0 + TC, T - HALO), HALO)
            prev = x_ref[pl.ds(p0, HALO), :].astype(F32) * jnp.where(seg_first, 0.0, 1.0)
            nxt = x_ref[pl.ds(n0, HALO), :].astype(F32) * jnp.where(seg_last, 0.0, 1.0)
            main = x_ref[pl.ds(t0, TC), :].astype(F32)
            xc = _conv_rows(jnp.concatenate([prev, main, nxt], axis=0), taps, TC) + cb_ref[...]
            xb = xc.astype(BF16)
            r = _sigmoid(jnp.dot(xb, wa_ref[...], preferred_element_type=F32) + ba_ref[...])
            gi = _sigmoid(jnp.dot(xb, wx_ref[...], preferred_element_type=F32) + bx_ref[...])
            log_a = (-RNN_C) * r * softplus
            a = jnp.exp(log_a)
            bb = jnp.sqrt(1.0 - a * a) * (gi * xc)
            a3 = a.reshape(G, SUBLANES, cb)
            b3 = bb.reshape(G, SUBLANES, cb)
            for s in (1, 2, 4):
                shift = s if fwd else SUBLANES - s
                use = (sub >= s) if fwd else (sub < SUBLANES - s)
                a_n, b_n = pltpu.roll(a3, shift, 1), pltpu.roll(b3, shift, 1)
                b3 = jnp.where(use, a3 * b_n + b3, b3)
                a3 = jnp.where(use, a3 * a_n, a3)
            a_scr[...] = a3
            b_scr[...] = b3

            def group(k, hc):
                g = k if fwd else G - 1 - k
                hg = b_scr[g] + a_scr[g] * hc
                b_scr[g] = hg
                return hg[SUBLANES - 1:SUBLANES] if fwd else hg[0:1]

            carry = lax.fori_loop(0, G, group, carry)
            o_ref[pl.ds(t0, TC), :] = b_scr[...].reshape(TC, cb).astype(o_ref.dtype)
            return carry

        lax.fori_loop(0, n_chunks, chunk, jnp.zeros((1, cb), F32))

    d = pl.program_id(2)
    pl.when(d == 0)(functools.partial(run, True))
    pl.when(d == 1)(functools.partial(run, False))


def _rglru(p3, conv_w, conv_b, wa, ba, wx, bx, lam, cl):
    B, T, _ = p3.shape
    D = conv_w.shape[1]
    nb = D // MXU_DIM
    vec = lambda: pl.BlockSpec((None, 1, MXU_DIM), lambda b, c, d: (d, 0, c))
    mat = lambda: pl.BlockSpec((None, None, MXU_DIM, MXU_DIM), lambda b, c, d: (d, c, 0, 0))
    G = RNN_CHUNK // SUBLANES
    return pl.pallas_call(
        functools.partial(_rglru_kernel, T=T, CL=cl, TC=RNN_CHUNK),
        grid=(B, nb, 2),
        in_specs=[pl.BlockSpec((None, T, MXU_DIM), lambda b, c, d: (b, 0, COL_RX * nb + c)),
                  pl.BlockSpec((RNN_CONV, MXU_DIM), lambda b, c, d: (0, c)),
                  pl.BlockSpec((1, MXU_DIM), lambda b, c, d: (0, c)),
                  mat(), vec(), mat(), vec(), vec()],
        out_specs=pl.BlockSpec((None, None, T, MXU_DIM), lambda b, c, d: (d, b, 0, c)),
        out_shape=jax.ShapeDtypeStruct((2, B, T, D), BF16),
        scratch_shapes=[pltpu.VMEM((G, SUBLANES, MXU_DIM), F32),
                        pltpu.VMEM((G, SUBLANES, MXU_DIM), F32)],
        compiler_params=_params("parallel", "parallel", "arbitrary"),
        name="rglru",
    )(p3, conv_w, conv_b.reshape(1, D), wa, ba.reshape(2, 1, D), wx, bx.reshape(2, 1, D),
      lam.reshape(2, 1, D))


def _attn_kernel(q_ref, k_ref, v_ref, lq_ref, g_ref, o_ref, vt_scr, sa0, sa1, sb0, sb1, *, T, CL, lam_init, do_ctx):
    TQ = ATTN_TQ
    n_lat = (T - CL) // TQ
    for c in range(T // TQ):
        vt_scr[:, c * TQ:(c + 1) * TQ] = v_ref[c * TQ:(c + 1) * TQ, :].T

    lq = lq_ref[...]
    lam = (jnp.exp(jnp.sum(lq[0:1] * lq[1:2], axis=1, keepdims=True))
           - jnp.exp(jnp.sum(lq[2:3] * lq[3:4], axis=1, keepdims=True)) + lam_init)

    def step(key_rows, fin, nxt):
        if nxt is not None:
            r_n, s_n = nxt
            qt = (q_ref[pl.ds(r_n, TQ), :].astype(F32) * (HEAD_DIM ** -0.5 * math.log2(math.e))).T
            row = lax.broadcasted_iota(jnp.int32, qt.shape, 0)
            map0 = (row % HEAD_DIM) < (HEAD_DIM // 2)
            qts = (jnp.where(map0, qt, 0.0).astype(BF16), jnp.where(map0, 0.0, qt).astype(BF16))
            m_n = [jnp.full((1, TQ), -1e30, F32)] * 2
        if fin is not None:
            r_f, s_f, m_f = fin
            l_f = [jnp.zeros((1, TQ), F32)] * 2
            acc = [jnp.zeros((VALUE_DIM, TQ), F32)] * 2
        for k0 in range(0, key_rows, ATTN_KC):
            size = min(ATTN_KC, key_rows - k0)
            if fin is not None:
                vt = vt_scr[:, k0:k0 + size]
                for i in range(2):
                    p = jnp.exp2(s_f[i][k0:k0 + size, :] - m_f[i])
                    l_f[i] = l_f[i] + jnp.sum(p, axis=0, keepdims=True)
                    acc[i] = acc[i] + jnp.dot(vt, p.astype(BF16), preferred_element_type=F32)
            if nxt is not None:
                kt = k_ref[k0:k0 + size, :]
                for i in range(2):
                    s = jnp.dot(kt, qts[i], preferred_element_type=F32)
                    s_n[i][k0:k0 + size, :] = s
                    m_n[i] = jnp.maximum(m_n[i], jnp.max(s, axis=0, keepdims=True))
        if fin is not None:
            o = acc[0] / l_f[0] - lam * (acc[1] / l_f[1])
            y = o * lax.rsqrt(jnp.mean(o * o, axis=0, keepdims=True) + NORM_EPS) * g_ref[...]
            o_ref[pl.ds(r_f, TQ), :] = (y * (1.0 - lam_init)).T.astype(o_ref.dtype)
        return tuple(m_n) if nxt is not None else None

    buf_a, buf_b = (sa0, sa1), (sb0, sb1)
    if do_ctx:
        for t in range(CL // TQ):
            m = step(CL, None, (t * TQ, buf_a))
            step(CL, (t * TQ, buf_a, m), None)
    else:
        o_ref[0:CL, :] = jnp.zeros((CL, VALUE_DIM), o_ref.dtype)

    def pair(jj, ma):
        r = pl.multiple_of(CL + 2 * jj * TQ, TQ)
        mb = step(T, (r, buf_a, ma), (r + TQ, buf_b))
        return step(T, (r + TQ, buf_b, mb), (r + 2 * TQ, buf_a))

    ma = lax.fori_loop(0, n_lat // 2 - 1, pair, step(T, None, (CL, buf_a)))
    r = T - 2 * TQ
    mb = step(T, (r, buf_a, ma), (r + TQ, buf_b))
    step(T, (r + TQ, buf_b, mb), None)


def _attention(p3, lq, g_subln, cl, lam_init, do_ctx):
    B, T, _ = p3.shape
    D = ATTN_HEADS * VALUE_DIM
    cpb = D // VALUE_DIM
    seq = lambda col: pl.BlockSpec((None, T, VALUE_DIM), lambda b, h: (b, 0, col * cpb + h))
    return pl.pallas_call(
        functools.partial(_attn_kernel, T=T, CL=cl, lam_init=lam_init, do_ctx=do_ctx),
        grid=(B, ATTN_HEADS),
        in_specs=[seq(COL_Q), seq(COL_K), seq(COL_V),
                  pl.BlockSpec((4, HEAD_DIM), lambda b, h: (0, 0)),
                  pl.BlockSpec((VALUE_DIM, 1), lambda b, h: (0, 0))],
        out_specs=pl.BlockSpec((None, T, VALUE_DIM), lambda b, h: (b, 0, h)),
        out_shape=jax.ShapeDtypeStruct((B, T, D), BF16),
        scratch_shapes=[pltpu.VMEM((VALUE_DIM, T), BF16)] + [pltpu.VMEM((T, ATTN_TQ), F32)] * 4,
        compiler_params=_params("parallel", "parallel"),
        name="diff_attention",
    )(p3, p3, p3, lq, g_subln.reshape(VALUE_DIM, 1))


def _conv_kernel(v_ref, vp_ref, vn_ref, g_ref, gp_ref, gn_ref, w_ref, b_ref, lg_ref, lb_ref, o_ref,
                 *, T, CL, TC):
    i = pl.program_id(1)
    n_ctx, n_chunks = CL // TC, T // TC
    seg_first = jnp.logical_or(i == 0, i == n_ctx)
    seg_last = jnp.logical_or(i == n_ctx - 1, i == n_chunks - 1)

    def gated(v, g):
        return v[...].astype(F32) * _sigmoid(g[...].astype(F32))

    ext = jnp.concatenate([gated(vp_ref, gp_ref) * jnp.where(seg_first, 0.0, 1.0),
                           gated(v_ref, g_ref),
                           gated(vn_ref, gn_ref) * jnp.where(seg_last, 0.0, 1.0)], axis=0)
    w = w_ref[...]
    cols = []
    for c in range(ext.shape[1] // LANES):
        sl = slice(c * LANES, (c + 1) * LANES)
        taps = [(k - (CONV_K - 1) // 2, w[k:k + 1, sl]) for k in range(CONV_K)]
        cols.append(_conv_rows(ext[:, sl], taps, TC))
    z = jnp.concatenate(cols, axis=1) + b_ref[...]
    mu = jnp.mean(z, axis=-1, keepdims=True)
    zc = z - mu
    var = jnp.mean(zc * zc, axis=-1, keepdims=True)
    y = zc * lax.rsqrt(var + NORM_EPS) * lg_ref[...] + lb_ref[...]
    o_ref[...] = _silu(y).astype(o_ref.dtype)


def _conformer_conv(p3, dw_w, dw_b, ln_g, ln_b, cl):
    B, T, _ = p3.shape
    D = dw_w.shape[1]
    TC = ROW_TILE
    hb = TC // HALO
    last = T // HALO - 1

    def main(col):
        return pl.BlockSpec((None, TC, D), lambda b, i: (b, i, col))

    def prev(col):
        return pl.BlockSpec((None, HALO, D), lambda b, i: (b, jnp.maximum(i * hb - 1, 0), col))

    def nxt(col):
        return pl.BlockSpec((None, HALO, D), lambda b, i: (b, jnp.minimum((i + 1) * hb, last), col))

    vec = lambda: pl.BlockSpec((1, D), lambda b, i: (0, 0))
    return pl.pallas_call(
        functools.partial(_conv_kernel, T=T, CL=cl, TC=TC),
        grid=(B, T // TC),
        in_specs=[main(COL_CV), prev(COL_CV), nxt(COL_CV), main(COL_CG), prev(COL_CG), nxt(COL_CG),
                  pl.BlockSpec((CONV_K, D), lambda b, i: (0, 0)), vec(), vec(), vec()],
        out_specs=pl.BlockSpec((None, TC, D), lambda b, i: (b, i, 0)),
        out_shape=jax.ShapeDtypeStruct((B, T, D), BF16),
        compiler_params=_params("parallel", "parallel"),
        name="conformer_conv",
    )(p3, p3, p3, p3, p3, p3, dw_w, dw_b.reshape(1, D), ln_g.reshape(1, D), ln_b.reshape(1, D))


def _merge_kernel(hf_ref, hb_ref, rg_ref, oa_ref, zc_ref, g0_ref, g1_ref, g2_ref, h_ref, m_ref,
                  wr_ref, wa_ref, wc_ref, wo_ref, o_ref):
    f32 = lambda r: r[...].astype(F32)
    rec = (f32(hf_ref) + f32(hb_ref)) * _gelu_tanh(f32(rg_ref))
    y_r = jnp.dot(rec.astype(BF16), wr_ref[...], preferred_element_type=F32)
    y_a = jnp.dot(oa_ref[...], wa_ref[...], preferred_element_type=F32)
    y_c = jnp.dot(zc_ref[...], wc_ref[...], preferred_element_type=F32)
    mix = _sigmoid(f32(g0_ref)) * y_r + _sigmoid(f32(g1_ref)) * y_a + _sigmoid(f32(g2_ref)) * y_c
    upd = jnp.dot(mix.astype(BF16), wo_ref[...], preferred_element_type=F32)
    o_ref[...] = h_ref[...] + m_ref[...][2:3] * upd


def _merge(hdir, p3, oa, zc, h, modtab, w_r, w_a, w_c, w_o, cl, first_tile):
    B, T, D = h.shape
    seg_of = _seg_index(cl // ROW_TILE)
    seg = lambda i: seg_of(i + first_tile)
    tile = lambda col: pl.BlockSpec((None, ROW_TILE, D), lambda b, i: (b, i + first_tile, col))
    wspec = lambda: pl.BlockSpec((D, D), lambda b, i: (0, 0))
    return pl.pallas_call(
        _merge_kernel,
        grid=(B, T // ROW_TILE - first_tile),
        in_specs=[pl.BlockSpec((None, None, ROW_TILE, D), lambda b, i: (0, b, i + first_tile, 0)),
                  pl.BlockSpec((None, None, ROW_TILE, D), lambda b, i: (1, b, i + first_tile, 0)),
                  tile(COL_RG), tile(0), tile(0), tile(COL_G), tile(COL_G + 1), tile(COL_G + 2),
                  tile(0),
                  pl.BlockSpec((None, None, 8, D), lambda b, i: (b, seg(i), 0, 0)),
                  wspec(), wspec(), wspec(), wspec()],
        out_specs=tile(0),
        out_shape=jax.ShapeDtypeStruct((B, T, D), F32),
        input_output_aliases={8: 0},
        compiler_params=_params("parallel", "parallel"),
        name="merge",
    )(hdir, hdir, p3, oa, zc, p3, p3, p3, h, modtab, w_r, w_a, w_c, w_o)


def _ffn_kernel(h_ref, g_ref, m_ref, wi_ref, wo_ref, o_ref, *, d_ff):
    h = h_ref[...]
    m = m_ref[...]
    u = _mod_norm(h, g_ref[...], m[3:4], m[4:5]).astype(BF16)
    gu = jnp.dot(u, wi_ref[...], preferred_element_type=F32)
    act = (_silu(gu[:, :d_ff]) * gu[:, d_ff:]).astype(BF16)
    o_ref[...] = h + m[5:6] * jnp.dot(act, wo_ref[...], preferred_element_type=F32)


def _ffn(h, g, modtab, w_i, w_o, cl, first_tile):
    B, T, D = h.shape
    d_ff = w_o.shape[0]
    seg_of = _seg_index(cl // ROW_TILE)
    seg = lambda i: seg_of(i + first_tile)
    tile = pl.BlockSpec((None, ROW_TILE, D), lambda b, i: (b, i + first_tile, 0))
    return pl.pallas_call(
        functools.partial(_ffn_kernel, d_ff=d_ff),
        grid=(B, T // ROW_TILE - first_tile),
        in_specs=[tile,
                  pl.BlockSpec((1, D), lambda b, i: (0, 0)),
                  pl.BlockSpec((None, None, 8, D), lambda b, i: (b, seg(i), 0, 0)),
                  pl.BlockSpec((D, 2 * d_ff), lambda b, i: (0, 0)),
                  pl.BlockSpec((d_ff, D), lambda b, i: (0, 0))],
        out_specs=tile,
        out_shape=jax.ShapeDtypeStruct((B, T, D), F32),
        input_output_aliases={0: 0},
        compiler_params=_params("parallel", "parallel"),
        name="ffn",
    )(h, g.reshape(1, D), modtab, w_i, w_o)


def _final_kernel(h_ref, g_ref, o_ref):
    x = h_ref[...]
    o_ref[...] = x * lax.rsqrt(jnp.mean(x * x, axis=-1, keepdims=True) + NORM_EPS) * g_ref[...]


def _final_norm(h, g, cl):
    B, T, D = h.shape
    S = T - cl
    skip = cl // ROW_TILE
    return pl.pallas_call(
        _final_kernel,
        grid=(B, S // ROW_TILE),
        in_specs=[pl.BlockSpec((None, ROW_TILE, D), lambda b, i: (b, i + skip, 0)),
                  pl.BlockSpec((1, D), lambda b, i: (0, 0))],
        out_specs=pl.BlockSpec((None, ROW_TILE, D), lambda b, i: (b, i, 0)),
        out_shape=jax.ShapeDtypeStruct((B, S, D), F32),
        compiler_params=_params("parallel", "parallel"),
        name="final_norm",
    )(h, g.reshape(1, D))


def _rope_tables(cl, s):
    pairs_axis = HEAD_DIM // 4
    rows = jnp.repeat(jnp.arange(s // GRID_W, dtype=F32), GRID_W)
    cols = jnp.tile(jnp.arange(GRID_W, dtype=F32), s // GRID_W)
    inv = ROPE_BASE ** (-jnp.arange(pairs_axis, dtype=F32) / pairs_axis)
    ang = jnp.concatenate([rows[:, None] * inv, cols[:, None] * inv], axis=-1)
    cos = jnp.concatenate([jnp.ones((cl, HEAD_DIM // 2), F32), jnp.cos(ang)], axis=0)
    sin = jnp.concatenate([jnp.zeros((cl, HEAD_DIM // 2), F32), jnp.sin(ang)], axis=0)
    return (jnp.concatenate([cos, cos, cos, cos], axis=1),
            jnp.concatenate([-sin, -sin, sin, sin], axis=1))


def _rope_layout(a):
    lead = a.shape[:-1]
    a = a.reshape(lead + (ATTN_HEADS, 2, 2, HEAD_DIM // 2))
    return jnp.swapaxes(a, -3, -2).reshape(lead + (ATTN_HEADS * VALUE_DIM,))


def _permute_qk_columns(a):
    d = a.shape[-1] // N_COL_BLOCKS
    blocks = [a[..., i * d:(i + 1) * d] for i in range(N_COL_BLOCKS)]
    for i in (COL_K, COL_Q):
        blocks[i] = _rope_layout(blocks[i])
    return jnp.concatenate(blocks, axis=-1)


def _blockdiag_tiles(w):
    two, nb, bs, _ = w.shape
    per = MXU_DIM // bs
    w = w.reshape(two, nb // per, per, bs, bs)
    eye = jnp.eye(per, dtype=w.dtype)
    t = jnp.einsum('dtpio,pq->dtpiqo', w, eye)
    return t.reshape(two, nb // per, MXU_DIM, MXU_DIM)


def kernel(x, c, ctx, c_ctx, w_mod, b_mod, g_norm1, g_norm2, w_in, b_in, rnn_conv_w, rnn_conv_b, rnn_w_a, rnn_b_a, rnn_w_x, rnn_b_x, rnn_lambda, w_rnn_o, lambda_qk, g_subln, w_attn_o, conv_dw_w, conv_dw_b, conv_ln_g, conv_ln_b, w_conv_o, w_out, w_ffn_in, w_ffn_out, g_final):
    B, S, D = x.shape
    CL = ctx.shape[1]
    T = CL + S
    L = w_mod.shape[0]
    assert D == ATTN_HEADS * VALUE_DIM and S % GRID_W == 0
    assert CL % ROW_TILE == 0 and S % ROW_TILE == 0 and ROW_TILE == ATTN_TQ
    assert (S // ATTN_TQ) % 2 == 0 and CL % ATTN_KC == 0 and S % ATTN_KC == 0
    assert T % 4 == 0 and (T // 4) % BF16_ROWS == 0

    h = jnp.concatenate([ctx, x], axis=1)

    n_cond = -(-(B + 1) // SUBLANES) * SUBLANES
    cc = jnp.zeros((n_cond, D), F32).at[:B].set(c).at[B].set(c_ctx)
    mod = _modulation(cc, w_mod, b_mod).reshape(L, n_cond, N_MOD, D)
    mod_ctx = jnp.broadcast_to(mod[:, B][:, None], (L, B, N_MOD, D))
    modtab = jnp.stack([mod_ctx, mod[:, :B]], axis=2)
    modtab = jnp.pad(modtab, ((0, 0), (0, 0), (0, 0), (0, 8 - N_MOD), (0, 0)))

    cos_t, sin_t = _rope_tables(CL, S)
    w_in_b = _permute_qk_columns(w_in).astype(BF16)
    b_in_p = _permute_qk_columns(b_in)

    for l in range(L):
        lam_init = 0.8 - 0.6 * math.exp(-0.3 * l)
        first_tile = CL // ROW_TILE if l == L - 1 else 0
        p = _in_proj(h.reshape(B * T, D), g_norm1[l], modtab[l], w_in_b[l], b_in_p[l], cos_t, sin_t, CL, T // 4)
        p3 = p.reshape(B, T, N_COL_BLOCKS * D)
        hdir = _rglru(p3, rnn_conv_w[l], rnn_conv_b[l],
                      _blockdiag_tiles(rnn_w_a[l]).astype(BF16), rnn_b_a[l],
                      _blockdiag_tiles(rnn_w_x[l]).astype(BF16), rnn_b_x[l], rnn_lambda[l], CL)
        oa = _attention(p3, lambda_qk[l], g_subln[l], CL, lam_init, first_tile == 0)
        zc = _conformer_conv(p3, conv_dw_w[l], conv_dw_b[l], conv_ln_g[l], conv_ln_b[l], CL)
        h = _merge(hdir, p3, oa, zc, h, modtab[l], w_rnn_o[l].astype(BF16), w_attn_o[l].astype(BF16),
                   w_conv_o[l].astype(BF16), w_out[l].astype(BF16), CL, first_tile)
        h = _ffn(h, g_norm2[l], modtab[l], w_ffn_in[l].astype(BF16), w_ffn_out[l].astype(BF16), CL, first_tile)
    return _final_norm(h, g_final, CL)
```

```python
import functools
import math

import jax
import jax.numpy as jnp
from jax import lax
from jax.experimental import pallas as pl
from jax.experimental.pallas import tpu as pltpu

F32 = jnp.float32
BF16 = jnp.bfloat16

NORM_EPS = 1e-6
N_MOD = 6
ATTN_HEADS = 8
HEAD_DIM = 64
VALUE_DIM = 2 * HEAD_DIM
GRID_W = 64
ROPE_BASE = 10000.0
RNN_BLOCK = 64
RNN_CONV = 4
RNN_C = 8.0
CONV_K = 31
LANES = 128
SUBLANES = 8
BF16_ROWS = 16
HALO = 16
MXU_DIM = 256
VMEM_LIMIT = 56 * 1024 * 1024

COL_RX, COL_K, COL_V, COL_RG, COL_Q, COL_CV, COL_CG, COL_G = 0, 1, 2, 3, 4, 5, 6, 7
N_COL_BLOCKS = 10

ROW_TILE = 256
RNN_CHUNK = 256
RNN_UNROLL = 4
ATTN_TQ = 256
ATTN_KC = 256
PROJ_SPLIT = 4


def _sigmoid(v):
    return 1.0 / (1.0 + jnp.exp(-v))


def _silu(v):
    return v * _sigmoid(v)


def _gelu_tanh(v):
    return 0.5 * v * (1.0 + jnp.tanh(math.sqrt(2.0 / math.pi) * (v + 0.044715 * (v * v * v))))


def _params(*sem):
    return pltpu.CompilerParams(dimension_semantics=sem, vmem_limit_bytes=VMEM_LIMIT)


def _mod_kernel(c_ref, w_ref, b_ref, o_ref):
    s = _silu(c_ref[...])
    o_ref[...] = jnp.dot(s, w_ref[...], preferred_element_type=F32,
                         precision=lax.Precision.HIGHEST) + b_ref[...]


def _modulation(cc, w_mod, b_mod):
    L, D, _ = w_mod.shape
    R = cc.shape[0]
    return pl.pallas_call(
        _mod_kernel,
        grid=(L, N_MOD),
        in_specs=[pl.BlockSpec((R, D), lambda l, j: (0, 0)),
                  pl.BlockSpec((None, D, D), lambda l, j: (l, 0, j)),
                  pl.BlockSpec((None, 1, D), lambda l, j: (l, 0, j))],
        out_specs=pl.BlockSpec((None, R, D), lambda l, j: (l, 0, j)),
        out_shape=jax.ShapeDtypeStruct((L, R, N_MOD * D), F32),
        compiler_params=_params("arbitrary", "arbitrary"),
        name="modulation",
    )(cc, w_mod, b_mod.reshape(L, 1, N_MOD * D))


def _seg_index(n_ctx_tiles):
    return lambda i: jnp.where(i >= n_ctx_tiles, 1, 0)


def _mod_norm(x, g, shift, scale):
    y = x * lax.rsqrt(jnp.mean(x * x, axis=-1, keepdims=True) + NORM_EPS) * g
    return y * (1.0 + scale) + shift


def _rope(t, cos, sin_signed):
    outs = []
    for k in range(t.shape[1] // LANES):
        tk = t[:, k * LANES:(k + 1) * LANES]
        outs.append(tk * cos + pltpu.roll(tk, LANES // 2, 1) * sin_signed)
    return jnp.concatenate(outs, axis=1)


def _proj_kernel(h_ref, g_ref, m_ref, w_ref, b_ref, cos_ref, sin_ref, o_ref, u_scr, *, CL, tiles_per_seq):
    i, j = pl.program_id(0), pl.program_id(1)
    tm = h_ref.shape[0]

    @pl.when(j == 0)
    def _():
        m = m_ref[...]
        row = (i % tiles_per_seq) * tm + lax.broadcasted_iota(jnp.int32, (tm, 1), 0)
        is_ctx = row < CL
        shift = jnp.where(is_ctx, m[0, 0:1], m[1, 0:1])
        scale = jnp.where(is_ctx, m[0, 1:2], m[1, 1:2])
        u_scr[...] = _mod_norm(h_ref[...], g_ref[...], shift, scale).astype(u_scr.dtype)

    is_rope = jnp.logical_or(j == COL_K, j == COL_Q)
    rb = tm // PROJ_SPLIT

    def project(k):
        sl = slice(k * rb, (k + 1) * rb)
        return sl, jnp.dot(u_scr[sl, :], w_ref[...], preferred_element_type=F32) + b_ref[...]

    @pl.when(is_rope)
    def _():
        for k in range(PROJ_SPLIT):
            sl, acc = project(k)
            o_ref[sl, :] = _rope(acc, cos_ref[sl, :], sin_ref[sl, :]).astype(o_ref.dtype)

    @pl.when(jnp.logical_not(is_rope))
    def _():
        for k in range(PROJ_SPLIT):
            sl, acc = project(k)
            o_ref[sl, :] = acc.astype(o_ref.dtype)


def _in_proj(h2, g, modtab, w, b, cos_t, sin_t, cl, tm):
    M, D = h2.shape
    N = w.shape[1]
    T = cos_t.shape[0]
    tiles_per_seq = T // tm
    return pl.pallas_call(
        functools.partial(_proj_kernel, CL=cl, tiles_per_seq=tiles_per_seq),
        grid=(M // tm, N // D),
        in_specs=[pl.BlockSpec((tm, D), lambda i, j: (i, 0)),
                  pl.BlockSpec((1, D), lambda i, j: (0, 0)),
                  pl.BlockSpec((None, 2, 8, D), lambda i, j: (i // tiles_per_seq, 0, 0, 0)),
                  pl.BlockSpec((D, D), lambda i, j: (0, j)),
                  pl.BlockSpec((1, D), lambda i, j: (0, j)),
                  pl.BlockSpec((tm, LANES), lambda i, j: (i % tiles_per_seq, 0)),
                  pl.BlockSpec((tm, LANES), lambda i, j: (i % tiles_per_seq, 0))],
        out_specs=pl.BlockSpec((tm, D), lambda i, j: (i, j)),
        out_shape=jax.ShapeDtypeStruct((M, N), BF16),
        scratch_shapes=[pltpu.VMEM((tm, D), BF16)],
        compiler_params=_params("parallel", "arbitrary"),
        name="in_proj",
    )(h2, g.reshape(1, D), modtab, w, b.reshape(1, N), cos_t, sin_t)


def _conv_rows(ext, taps, rows_out):
    n = ext.shape[0]
    rolled = {}
    acc = None
    for off, w in taps:
        start = HALO + off
        r, q = start % SUBLANES, start // SUBLANES
        if r not in rolled:
            rolled[r] = ext if r == 0 else pltpu.roll(ext, n - r, 0)
        term = w * rolled[r][q * SUBLANES:q * SUBLANES + rows_out]
        acc = term if acc is None else acc + term
    return acc


def _rglru_kernel(x_ref, cw_ref, cb_ref, wa_ref, ba_ref, wx_ref, bx_ref, lam_ref, o_ref,
                  xc_scr, h_scr, *, T, CL, TC):
    n_chunks, n_ctx = T // TC, CL // TC
    G = TC // SUBLANES
    cb = x_ref.shape[1]
    cw = cw_ref[...]
    taps = [(k - 2, cw[k:k + 1]) for k in range(RNN_CONV)]
    sub = lax.broadcasted_iota(jnp.int32, (G, SUBLANES, cb), 1)

    def conv_chunk(ci, _):
        t0 = pl.multiple_of(ci * TC, TC)
        seg_first = jnp.logical_or(ci == 0, ci == n_ctx)
        seg_last = jnp.logical_or(ci == n_ctx - 1, ci == n_chunks - 1)
        p0 = pl.multiple_of(jnp.maximum(t0 - HALO, 0), HALO)
        n0 = pl.multiple_of(jnp.minimum(t0 + TC, T - HALO), HALO)
        prev = x_ref[pl.ds(p0, HALO), :].astype(F32) * jnp.where(seg_first, 0.0, 1.0)
        nxt = x_ref[pl.ds(n0, HALO), :].astype(F32) * jnp.where(seg_last, 0.0, 1.0)
        main = x_ref[pl.ds(t0, TC), :].astype(F32)
        xc_scr[pl.ds(t0, TC), :] = _conv_rows(jnp.concatenate([prev, main, nxt], axis=0), taps, TC) + cb_ref[...]
        return 0

    lax.fori_loop(0, n_chunks, conv_chunk, 0)

    def sweep(fwd):
        d = 0 if fwd else 1
        neg_lam = -lam_ref[d]
        softplus = jnp.maximum(neg_lam, 0.0) + jnp.log(1.0 + jnp.exp(-jnp.abs(neg_lam)))

        def chunk(step, carry):
            if fwd:
                ci = step
            else:
                ci = jnp.where(step < n_ctx, n_ctx - 1 - step, n_chunks - 1 - (step - n_ctx))
            t0 = pl.multiple_of(ci * TC, TC)
            xc = xc_scr[pl.ds(t0, TC), :]
            xb = xc.astype(BF16)
            r = _sigmoid(jnp.dot(xb, wa_ref[d], preferred_element_type=F32) + ba_ref[d])
            gi = _sigmoid(jnp.dot(xb, wx_ref[d], preferred_element_type=F32) + bx_ref[d])
            log_a = (-RNN_C) * r * softplus
            a = jnp.exp(log_a)
            bb = jnp.sqrt(1.0 - a * a) * (gi * xc)
            a3 = a.reshape(G, SUBLANES, cb)
            b3 = bb.reshape(G, SUBLANES, cb)
            for s in (1, 2, 4):
                shift = s if fwd else SUBLANES - s
                use = (sub >= s) if fwd else (sub < SUBLANES - s)
                a_n, b_n = pltpu.roll(a3, shift, 1), pltpu.roll(b3, shift, 1)
                b3 = jnp.where(use, a3 * b_n + b3, b3)
                a3 = jnp.where(use, a3 * a_n, a3)

            hs = [None] * G
            for k in range(G):
                g = k if fwd else G - 1 - k
                hs[g] = b3[g] + a3[g] * carry
                carry = hs[g][SUBLANES - 1:SUBLANES] if fwd else hs[g][0:1]
            h = jnp.concatenate(hs, axis=0)
            if fwd:
                h_scr[pl.ds(t0, TC), :] = h
            else:
                o_ref[pl.ds(t0, TC), :] = (h_scr[pl.ds(t0, TC), :] + h).astype(o_ref.dtype)
            return carry

        lax.fori_loop(0, n_chunks, chunk, jnp.zeros((1, cb), F32), unroll=RNN_UNROLL)

    sweep(True)
    sweep(False)


def _rglru(p3, conv_w, conv_b, wa, ba, wx, bx, lam, cl):
    B, T, _ = p3.shape
    D = conv_w.shape[1]
    nb = D // MXU_DIM
    vec = lambda: pl.BlockSpec((2, 1, MXU_DIM), lambda b, c: (0, 0, c))
    mat = lambda: pl.BlockSpec((2, None, MXU_DIM, MXU_DIM), lambda b, c: (0, c, 0, 0))
    return pl.pallas_call(
        functools.partial(_rglru_kernel, T=T, CL=cl, TC=RNN_CHUNK),
        grid=(B, nb),
        in_specs=[pl.BlockSpec((None, T, MXU_DIM), lambda b, c: (b, 0, COL_RX * nb + c)),
                  pl.BlockSpec((RNN_CONV, MXU_DIM), lambda b, c: (0, c)),
                  pl.BlockSpec((1, MXU_DIM), lambda b, c: (0, c)),
                  mat(), vec(), mat(), vec(), vec()],
        out_specs=pl.BlockSpec((None, T, MXU_DIM), lambda b, c: (b, 0, c)),
        out_shape=jax.ShapeDtypeStruct((B, T, D), BF16),
        scratch_shapes=[pltpu.VMEM((T, MXU_DIM), F32), pltpu.VMEM((T, MXU_DIM), F32)],
        compiler_params=_params("parallel", "parallel"),
        name="rglru",
    )(p3, conv_w, conv_b.reshape(1, D), wa, ba.reshape(2, 1, D), wx, bx.reshape(2, 1, D),
      lam.reshape(2, 1, D))


def _attn_kernel(q_ref, k_ref, v_ref, lq_ref, g_ref, o_ref, vt_scr, sa0, sa1, sb0, sb1, *, T, CL, lam_init, do_ctx):
    TQ = ATTN_TQ
    n_lat = (T - CL) // TQ
    for c in range(T // TQ):
        vt_scr[:, c * TQ:(c + 1) * TQ] = v_ref[c * TQ:(c + 1) * TQ, :].T

    lq = lq_ref[...]
    lam = (jnp.exp(jnp.sum(lq[0:1] * lq[1:2], axis=1, keepdims=True))
           - jnp.exp(jnp.sum(lq[2:3] * lq[3:4], axis=1, keepdims=True)) + lam_init)

    def step(key_rows, fin, nxt):
        if nxt is not None:
            r_n, s_n = nxt
            qt = (q_ref[pl.ds(r_n, TQ), :].astype(F32) * (HEAD_DIM ** -0.5 * math.log2(math.e))).T
            row = lax.broadcasted_iota(jnp.int32, qt.shape, 0)
            map0 = (row % HEAD_DIM) < (HEAD_DIM // 2)
            qts = (jnp.where(map0, qt, 0.0).astype(BF16), jnp.where(map0, 0.0, qt).astype(BF16))
            m_n = [jnp.full((1, TQ), -1e30, F32)] * 2
        if fin is not None:
            r_f, s_f, m_f = fin
            l_f = [jnp.zeros((1, TQ), F32)] * 2
            acc = [jnp.zeros((VALUE_DIM, TQ), F32)] * 2
        for k0 in range(0, key_rows, ATTN_KC):
            size = min(ATTN_KC, key_rows - k0)
            if fin is not None:
                vt = vt_scr[:, k0:k0 + size]
                for i in range(2):
                    p = jnp.exp2(s_f[i][k0:k0 + size, :] - m_f[i])
                    l_f[i] = l_f[i] + jnp.sum(p, axis=0, keepdims=True)
                    acc[i] = acc[i] + jnp.dot(vt, p.astype(BF16), preferred_element_type=F32)
            if nxt is not None:
                kt = k_ref[k0:k0 + size, :]
                for i in range(2):
                    s = jnp.dot(kt, qts[i], preferred_element_type=F32)
                    s_n[i][k0:k0 + size, :] = s
                    m_n[i] = jnp.maximum(m_n[i], jnp.max(s, axis=0, keepdims=True))
        if fin is not None:
            o = acc[0] / l_f[0] - lam * (acc[1] / l_f[1])
            y = o * lax.rsqrt(jnp.mean(o * o, axis=0, keepdims=True) + NORM_EPS) * g_ref[...]
            o_ref[pl.ds(r_f, TQ), :] = (y * (1.0 - lam_init)).T.astype(o_ref.dtype)
        return tuple(m_n) if nxt is not None else None

    buf_a, buf_b = (sa0, sa1), (sb0, sb1)
    if do_ctx:
        for t in range(CL // TQ):
            m = step(CL, None, (t * TQ, buf_a))
            step(CL, (t * TQ, buf_a, m), None)
    else:
        o_ref[0:CL, :] = jnp.zeros((CL, VALUE_DIM), o_ref.dtype)

    def pair(jj, ma):
        r = pl.multiple_of(CL + 2 * jj * TQ, TQ)
        mb = step(T, (r, buf_a, ma), (r + TQ, buf_b))
        return step(T, (r + TQ, buf_b, mb), (r + 2 * TQ, buf_a))

    ma = lax.fori_loop(0, n_lat // 2 - 1, pair, step(T, None, (CL, buf_a)))
    r = T - 2 * TQ
    mb = step(T, (r, buf_a, ma), (r + TQ, buf_b))
    step(T, (r + TQ, buf_b, mb), None)


def _attention(p3, lq, g_subln, cl, lam_init, do_ctx):
    B, T, _ = p3.shape
    D = ATTN_HEADS * VALUE_DIM
    cpb = D // VALUE_DIM
    seq = lambda col: pl.BlockSpec((None, T, VALUE_DIM), lambda b, h: (b, 0, col * cpb + h))
    return pl.pallas_call(
        functools.partial(_attn_kernel, T=T, CL=cl, lam_init=lam_init, do_ctx=do_ctx),
        grid=(B, ATTN_HEADS),
        in_specs=[seq(COL_Q), seq(COL_K), seq(COL_V),
                  pl.BlockSpec((4, HEAD_DIM), lambda b, h: (0, 0)),
                  pl.BlockSpec((VALUE_DIM, 1), lambda b, h: (0, 0))],
        out_specs=pl.BlockSpec((None, T, VALUE_DIM), lambda b, h: (b, 0, h)),
        out_shape=jax.ShapeDtypeStruct((B, T, D), BF16),
        scratch_shapes=[pltpu.VMEM((VALUE_DIM, T), BF16)] + [pltpu.VMEM((T, ATTN_TQ), F32)] * 4,
        compiler_params=_params("parallel", "parallel"),
        name="diff_attention",
    )(p3, p3, p3, lq, g_subln.reshape(VALUE_DIM, 1))


def _conv_kernel(v_ref, vp_ref, vn_ref, g_ref, gp_ref, gn_ref, w_ref, b_ref, lg_ref, lb_ref, o_ref,
                 *, T, CL, TC):
    i = pl.program_id(1)
    n_ctx, n_chunks = CL // TC, T // TC
    seg_first = jnp.logical_or(i == 0, i == n_ctx)
    seg_last = jnp.logical_or(i == n_ctx - 1, i == n_chunks - 1)

    def gated(v, g):
        return v[...].astype(F32) * _sigmoid(g[...].astype(F32))

    ext = jnp.concatenate([gated(vp_ref, gp_ref) * jnp.where(seg_first, 0.0, 1.0),
                           gated(v_ref, g_ref),
                           gated(vn_ref, gn_ref) * jnp.where(seg_last, 0.0, 1.0)], axis=0)
    w = w_ref[...]
    cols = []
    for c in range(ext.shape[1] // LANES):
        sl = slice(c * LANES, (c + 1) * LANES)
        taps = [(k - (CONV_K - 1) // 2, w[k:k + 1, sl]) for k in range(CONV_K)]
        cols.append(_conv_rows(ext[:, sl], taps, TC))
    z = jnp.concatenate(cols, axis=1) + b_ref[...]
    mu = jnp.mean(z, axis=-1, keepdims=True)
    zc = z - mu
    var = jnp.mean(zc * zc, axis=-1, keepdims=True)
    y = zc * lax.rsqrt(var + NORM_EPS) * lg_ref[...] + lb_ref[...]
    o_ref[...] = _silu(y).astype(o_ref.dtype)


def _conformer_conv(p3, dw_w, dw_b, ln_g, ln_b, cl):
    B, T, _ = p3.shape
    D = dw_w.shape[1]
    TC = ROW_TILE
    hb = TC // HALO
    last = T // HALO - 1

    def main(col):
        return pl.BlockSpec((None, TC, D), lambda b, i: (b, i, col))

    def prev(col):
        return pl.BlockSpec((None, HALO, D), lambda b, i: (b, jnp.maximum(i * hb - 1, 0), col))

    def nxt(col):
        return pl.BlockSpec((None, HALO, D), lambda b, i: (b, jnp.minimum((i + 1) * hb, last), col))

    vec = lambda: pl.BlockSpec((1, D), lambda b, i: (0, 0))
    return pl.pallas_call(
        functools.partial(_conv_kernel, T=T, CL=cl, TC=TC),
        grid=(B, T // TC),
        in_specs=[main(COL_CV), prev(COL_CV), nxt(COL_CV), main(COL_CG), prev(COL_CG), nxt(COL_CG),
                  pl.BlockSpec((CONV_K, D), lambda b, i: (0, 0)), vec(), vec(), vec()],
        out_specs=pl.BlockSpec((None, TC, D), lambda b, i: (b, i, 0)),
        out_shape=jax.ShapeDtypeStruct((B, T, D), BF16),
        compiler_params=_params("parallel", "parallel"),
        name="conformer_conv",
    )(p3, p3, p3, p3, p3, p3, dw_w, dw_b.reshape(1, D), ln_g.reshape(1, D), ln_b.reshape(1, D))


def _merge_kernel(hr_ref, rg_ref, oa_ref, zc_ref, g0_ref, g1_ref, g2_ref, h_ref, m_ref,
                  wr_ref, wa_ref, wc_ref, wo_ref, o_ref):
    f32 = lambda r: r[...].astype(F32)
    rec = f32(hr_ref) * _gelu_tanh(f32(rg_ref))
    y_r = jnp.dot(rec.astype(BF16), wr_ref[...], preferred_element_type=F32)
    y_a = jnp.dot(oa_ref[...], wa_ref[...], preferred_element_type=F32)
    y_c = jnp.dot(zc_ref[...], wc_ref[...], preferred_element_type=F32)
    mix = _sigmoid(f32(g0_ref)) * y_r + _sigmoid(f32(g1_ref)) * y_a + _sigmoid(f32(g2_ref)) * y_c
    upd = jnp.dot(mix.astype(BF16), wo_ref[...], preferred_element_type=F32)
    o_ref[...] = h_ref[...] + m_ref[...][2:3] * upd


def _merge(hr, p3, oa, zc, h, modtab, w_r, w_a, w_c, w_o, cl, first_tile):
    B, T, D = h.shape
    seg_of = _seg_index(cl // ROW_TILE)
    seg = lambda i: seg_of(i + first_tile)
    tile = lambda col: pl.BlockSpec((None, ROW_TILE, D), lambda b, i: (b, i + first_tile, col))
    wspec = lambda: pl.BlockSpec((D, D), lambda b, i: (0, 0))
    return pl.pallas_call(
        _merge_kernel,
        grid=(B, T // ROW_TILE - first_tile),
        in_specs=[tile(0), tile(COL_RG), tile(0), tile(0), tile(COL_G), tile(COL_G + 1), tile(COL_G + 2),
                  tile(0),
                  pl.BlockSpec((None, None, 8, D), lambda b, i: (b, seg(i), 0, 0)),
                  wspec(), wspec(), wspec(), wspec()],
        out_specs=tile(0),
        out_shape=jax.ShapeDtypeStruct((B, T, D), F32),
        input_output_aliases={7: 0},
        compiler_params=_params("parallel", "parallel"),
        name="merge",
    )(hr, p3, oa, zc, p3, p3, p3, h, modtab, w_r, w_a, w_c, w_o)


def _ffn_update(h_ref, g_ref, m_ref, wi_ref, wo_ref, d_ff):
    h = h_ref[...]
    m = m_ref[...]
    u = _mod_norm(h, g_ref[...], m[3:4], m[4:5]).astype(BF16)
    gu = jnp.dot(u, wi_ref[...], preferred_element_type=F32)
    act = (_silu(gu[:, :d_ff]) * gu[:, d_ff:]).astype(BF16)
    return h + m[5:6] * jnp.dot(act, wo_ref[...], preferred_element_type=F32)


def _ffn_kernel(h_ref, g_ref, m_ref, wi_ref, wo_ref, o_ref, *, d_ff):
    o_ref[...] = _ffn_update(h_ref, g_ref, m_ref, wi_ref, wo_ref, d_ff)


def _ffn_final_kernel(h_ref, g_ref, m_ref, wi_ref, wo_ref, gf_ref, o_ref, *, d_ff):
    x = _ffn_update(h_ref, g_ref, m_ref, wi_ref, wo_ref, d_ff)
    o_ref[...] = x * lax.rsqrt(jnp.mean(x * x, axis=-1, keepdims=True) + NORM_EPS) * gf_ref[...]


def _ffn(h, g, modtab, w_i, w_o, cl, first_tile, g_final=None):
    B, T, D = h.shape
    d_ff = w_o.shape[0]
    seg_of = _seg_index(cl // ROW_TILE)
    seg = lambda i: seg_of(i + first_tile)
    tile = pl.BlockSpec((None, ROW_TILE, D), lambda b, i: (b, i + first_tile, 0))
    vec = pl.BlockSpec((1, D), lambda b, i: (0, 0))
    in_specs = [tile, vec,
                pl.BlockSpec((None, None, 8, D), lambda b, i: (b, seg(i), 0, 0)),
                pl.BlockSpec((D, 2 * d_ff), lambda b, i: (0, 0)),
                pl.BlockSpec((d_ff, D), lambda b, i: (0, 0))]
    grid = (B, T // ROW_TILE - first_tile)
    if g_final is None:
        return pl.pallas_call(
            functools.partial(_ffn_kernel, d_ff=d_ff), grid=grid, in_specs=in_specs, out_specs=tile,
            out_shape=jax.ShapeDtypeStruct((B, T, D), F32), input_output_aliases={0: 0},
            compiler_params=_params("parallel", "parallel"), name="ffn",
        )(h, g.reshape(1, D), modtab, w_i, w_o)
    assert first_tile * ROW_TILE == cl
    return pl.pallas_call(
        functools.partial(_ffn_final_kernel, d_ff=d_ff), grid=grid, in_specs=in_specs + [vec],
        out_specs=pl.BlockSpec((None, ROW_TILE, D), lambda b, i: (b, i, 0)),
        out_shape=jax.ShapeDtypeStruct((B, T - cl, D), F32),
        compiler_params=_params("parallel", "parallel"), name="ffn_final",
    )(h, g.reshape(1, D), modtab, w_i, w_o, g_final.reshape(1, D))


def _rope_tables(cl, s):
    pairs_axis = HEAD_DIM // 4
    rows = jnp.repeat(jnp.arange(s // GRID_W, dtype=F32), GRID_W)
    cols = jnp.tile(jnp.arange(GRID_W, dtype=F32), s // GRID_W)
    inv = ROPE_BASE ** (-jnp.arange(pairs_axis, dtype=F32) / pairs_axis)
    ang = jnp.concatenate([rows[:, None] * inv, cols[:, None] * inv], axis=-1)
    cos = jnp.concatenate([jnp.ones((cl, HEAD_DIM // 2), F32), jnp.cos(ang)], axis=0)
    sin = jnp.concatenate([jnp.zeros((cl, HEAD_DIM // 2), F32), jnp.sin(ang)], axis=0)
    return (jnp.concatenate([cos, cos, cos, cos], axis=1),
            jnp.concatenate([-sin, -sin, sin, sin], axis=1))


def _rope_layout(a):
    lead = a.shape[:-1]
    a = a.reshape(lead + (ATTN_HEADS, 2, 2, HEAD_DIM // 2))
    return jnp.swapaxes(a, -3, -2).reshape(lead + (ATTN_HEADS * VALUE_DIM,))


def _permute_qk_columns(a):
    d = a.shape[-1] // N_COL_BLOCKS
    blocks = [a[..., i * d:(i + 1) * d] for i in range(N_COL_BLOCKS)]
    for i in (COL_K, COL_Q):
        blocks[i] = _rope_layout(blocks[i])
    return jnp.concatenate(blocks, axis=-1)


def _blockdiag_tiles(w):
    two, nb, bs, _ = w.shape
    per = MXU_DIM // bs
    w = w.reshape(two, nb // per, per, bs, bs)
    eye = jnp.eye(per, dtype=w.dtype)
    t = jnp.einsum('dtpio,pq->dtpiqo', w, eye)
    return t.reshape(two, nb // per, MXU_DIM, MXU_DIM)


def kernel(x, c, ctx, c_ctx, w_mod, b_mod, g_norm1, g_norm2, w_in, b_in, rnn_conv_w, rnn_conv_b, rnn_w_a, rnn_b_a, rnn_w_x, rnn_b_x, rnn_lambda, w_rnn_o, lambda_qk, g_subln, w_attn_o, conv_dw_w, conv_dw_b, conv_ln_g, conv_ln_b, w_conv_o, w_out, w_ffn_in, w_ffn_out, g_final):
    B, S, D = x.shape
    CL = ctx.shape[1]
    T = CL + S
    L = w_mod.shape[0]
    assert D == ATTN_HEADS * VALUE_DIM and S % GRID_W == 0
    assert CL % ROW_TILE == 0 and S % ROW_TILE == 0 and ROW_TILE == ATTN_TQ
    assert (S // ATTN_TQ) % 2 == 0 and CL % ATTN_KC == 0 and S % ATTN_KC == 0
    assert T % 4 == 0 and (T // 4) % (PROJ_SPLIT * BF16_ROWS) == 0

    h = jnp.concatenate([ctx, x], axis=1)

    n_cond = -(-(B + 1) // SUBLANES) * SUBLANES
    cc = jnp.zeros((n_cond, D), F32).at[:B].set(c).at[B].set(c_ctx)
    mod = _modulation(cc, w_mod, b_mod).reshape(L, n_cond, N_MOD, D)
    mod_ctx = jnp.broadcast_to(mod[:, B][:, None], (L, B, N_MOD, D))
    modtab = jnp.stack([mod_ctx, mod[:, :B]], axis=2)
    modtab = jnp.pad(modtab, ((0, 0), (0, 0), (0, 0), (0, 8 - N_MOD), (0, 0)))

    cos_t, sin_t = _rope_tables(CL, S)
    w_in_b = _permute_qk_columns(w_in).astype(BF16)
    b_in_p = _permute_qk_columns(b_in)

    for l in range(L):
        lam_init = 0.8 - 0.6 * math.exp(-0.3 * l)
        first_tile = CL // ROW_TILE if l == L - 1 else 0
        p = _in_proj(h.reshape(B * T, D), g_norm1[l], modtab[l], w_in_b[l], b_in_p[l], cos_t, sin_t, CL, T // 4)
        p3 = p.reshape(B, T, N_COL_BLOCKS * D)
        hr = _rglru(p3, rnn_conv_w[l], rnn_conv_b[l],
                    _blockdiag_tiles(rnn_w_a[l]).astype(BF16), rnn_b_a[l],
                    _blockdiag_tiles(rnn_w_x[l]).astype(BF16), rnn_b_x[l], rnn_lambda[l], CL)
        oa = _attention(p3, lambda_qk[l], g_subln[l], CL, lam_init, first_tile == 0)
        zc = _conformer_conv(p3, conv_dw_w[l], conv_dw_b[l], conv_ln_g[l], conv_ln_b[l], CL)
        h = _merge(hr, p3, oa, zc, h, modtab[l], w_rnn_o[l].astype(BF16), w_attn_o[l].astype(BF16),
                   w_conv_o[l].astype(BF16), w_out[l].astype(BF16), CL, first_tile)
        h = _ffn(h, g_norm2[l], modtab[l], w_ffn_in[l].astype(BF16), w_ffn_out[l].astype(BF16), CL, first_tile,
                 g_final if l == L - 1 else None)
    return h
```

```python
import functools
import math

import jax
import jax.numpy as jnp
from jax import lax
from jax.experimental import pallas as pl
from jax.experimental.pallas import tpu as pltpu

F32 = jnp.float32
BF16 = jnp.bfloat16

NORM_EPS = 1e-6
N_MOD = 6
ATTN_HEADS = 8
HEAD_DIM = 64
VALUE_DIM = 2 * HEAD_DIM
GRID_W = 64
ROPE_BASE = 10000.0
RNN_BLOCK = 64
RNN_CONV = 4
RNN_C = 8.0
CONV_K = 31
LANES = 128
SUBLANES = 8
BF16_ROWS = 16
HALO = 16
MXU_DIM = 256
VMEM_LIMIT = 56 * 1024 * 1024

COL_RX, COL_K, COL_V, COL_RG, COL_Q, COL_CV, COL_CG, COL_G = 0, 1, 2, 3, 4, 5, 6, 7
N_COL_BLOCKS = 10

ROW_TILE = 256
RNN_CHUNK = 256
RNN_UNROLL = 4
ATTN_TQ = 256
ATTN_KC = 256
PROJ_TILES = 2
PROJ_SPLIT = 4


def _sigmoid(v):
    return 1.0 / (1.0 + jnp.exp(-v))


def _silu(v):
    return v * _sigmoid(v)


def _gelu_tanh(v):
    return 0.5 * v * (1.0 + jnp.tanh(math.sqrt(2.0 / math.pi) * (v + 0.044715 * (v * v * v))))


def _params(*sem):
    return pltpu.CompilerParams(dimension_semantics=sem, vmem_limit_bytes=VMEM_LIMIT)


def _mod_kernel(c_ref, w_ref, b_ref, o_ref):
    s = _silu(c_ref[...])
    o_ref[...] = jnp.dot(s, w_ref[...], preferred_element_type=F32,
                         precision=lax.Precision.HIGHEST) + b_ref[...]


def _modulation(cc, w_mod, b_mod):
    L, D, _ = w_mod.shape
    R = cc.shape[0]
    return pl.pallas_call(
        _mod_kernel,
        grid=(L, N_MOD),
        in_specs=[pl.BlockSpec((R, D), lambda l, j: (0, 0)),
                  pl.BlockSpec((None, D, D), lambda l, j: (l, 0, j)),
                  pl.BlockSpec((None, 1, D), lambda l, j: (l, 0, j))],
        out_specs=pl.BlockSpec((None, R, D), lambda l, j: (l, 0, j)),
        out_shape=jax.ShapeDtypeStruct((L, R, N_MOD * D), F32),
        compiler_params=_params("arbitrary", "arbitrary"),
        name="modulation",
    )(cc, w_mod, b_mod.reshape(L, 1, N_MOD * D))


def _seg_index(n_ctx_tiles):
    return lambda i: jnp.where(i >= n_ctx_tiles, 1, 0)


def _mod_norm(x, g, shift, scale):
    y = x * lax.rsqrt(jnp.mean(x * x, axis=-1, keepdims=True) + NORM_EPS) * g
    return y * (1.0 + scale) + shift


def _rope(t, cos, sin_signed):
    outs = []
    for k in range(t.shape[1] // LANES):
        tk = t[:, k * LANES:(k + 1) * LANES]
        outs.append(tk * cos + pltpu.roll(tk, LANES // 2, 1) * sin_signed)
    return jnp.concatenate(outs, axis=1)


def _proj_kernel(h_ref, g_ref, m_ref, w_ref, b_ref, cos_ref, sin_ref, o_ref, u_scr, *, CL, tiles_per_seq, split):
    i, j = pl.program_id(0), pl.program_id(1)
    tm = h_ref.shape[0]

    @pl.when(j == 0)
    def _():
        m = m_ref[...]
        row = (i % tiles_per_seq) * tm + lax.broadcasted_iota(jnp.int32, (tm, 1), 0)
        is_ctx = row < CL
        shift = jnp.where(is_ctx, m[0, 0:1], m[1, 0:1])
        scale = jnp.where(is_ctx, m[0, 1:2], m[1, 1:2])
        u_scr[...] = _mod_norm(h_ref[...], g_ref[...], shift, scale).astype(u_scr.dtype)

    is_rope = jnp.logical_or(j == COL_K, j == COL_Q)
    rb = tm // split

    def project(k):
        sl = slice(k * rb, (k + 1) * rb)
        return sl, jnp.dot(u_scr[sl, :], w_ref[...], preferred_element_type=F32) + b_ref[...]

    @pl.when(is_rope)
    def _():
        for k in range(split):
            sl, acc = project(k)
            o_ref[sl, :] = _rope(acc, cos_ref[sl, :], sin_ref[sl, :]).astype(o_ref.dtype)

    @pl.when(jnp.logical_not(is_rope))
    def _():
        for k in range(split):
            sl, acc = project(k)
            o_ref[sl, :] = acc.astype(o_ref.dtype)


def _in_proj(h2, g, modtab, w, b, cos_t, sin_t, cl, tm, split):
    M, D = h2.shape
    N = w.shape[1]
    T = cos_t.shape[0]
    tiles_per_seq = T // tm
    assert T % tm == 0 and tm % (split * BF16_ROWS) == 0
    return pl.pallas_call(
        functools.partial(_proj_kernel, CL=cl, tiles_per_seq=tiles_per_seq, split=split),
        grid=(M // tm, N // D),
        in_specs=[pl.BlockSpec((tm, D), lambda i, j: (i, 0)),
                  pl.BlockSpec((1, D), lambda i, j: (0, 0)),
                  pl.BlockSpec((None, 2, 8, D), lambda i, j: (i // tiles_per_seq, 0, 0, 0)),
                  pl.BlockSpec((D, D), lambda i, j: (0, j)),
                  pl.BlockSpec((1, D), lambda i, j: (0, j)),
                  pl.BlockSpec((tm, LANES), lambda i, j: (i % tiles_per_seq, 0)),
                  pl.BlockSpec((tm, LANES), lambda i, j: (i % tiles_per_seq, 0))],
        out_specs=pl.BlockSpec((tm, D), lambda i, j: (i, j)),
        out_shape=jax.ShapeDtypeStruct((M, N), BF16),
        scratch_shapes=[pltpu.VMEM((tm, D), BF16)],
        compiler_params=_params("parallel", "arbitrary"),
        name="in_proj",
    )(h2, g.reshape(1, D), modtab, w, b.reshape(1, N), cos_t, sin_t)


def _conv_rows(ext, taps, rows_out):
    n = ext.shape[0]
    rolled = {}
    acc = None
    for off, w in taps:
        start = HALO + off
        r, q = start % SUBLANES, start // SUBLANES
        if r not in rolled:
            rolled[r] = ext if r == 0 else pltpu.roll(ext, n - r, 0)
        term = w * rolled[r][q * SUBLANES:q * SUBLANES + rows_out]
        acc = term if acc is None else acc + term
    return acc


def _rglru_kernel(x_ref, cw_ref, cb_ref, wa_ref, ba_ref, wx_ref, bx_ref, lam_ref, o_ref,
                  xc_scr, h_scr, *, T, CL, TC):
    n_chunks, n_ctx = T // TC, CL // TC
    G = TC // SUBLANES
    cb = x_ref.shape[1]
    cw = cw_ref[...]
    taps = [(k - 2, cw[k:k + 1]) for k in range(RNN_CONV)]
    sub = lax.broadcasted_iota(jnp.int32, (G, SUBLANES, cb), 1)

    def conv_chunk(ci, _):
        t0 = pl.multiple_of(ci * TC, TC)
        seg_first = jnp.logical_or(ci == 0, ci == n_ctx)
        seg_last = jnp.logical_or(ci == n_ctx - 1, ci == n_chunks - 1)
        p0 = pl.multiple_of(jnp.maximum(t0 - HALO, 0), HALO)
        n0 = pl.multiple_of(jnp.minimum(t0 + TC, T - HALO), HALO)
        prev = x_ref[pl.ds(p0, HALO), :].astype(F32) * jnp.where(seg_first, 0.0, 1.0)
        nxt = x_ref[pl.ds(n0, HALO), :].astype(F32) * jnp.where(seg_last, 0.0, 1.0)
        main = x_ref[pl.ds(t0, TC), :].astype(F32)
        xc_scr[pl.ds(t0, TC), :] = _conv_rows(jnp.concatenate([prev, main, nxt], axis=0), taps, TC) + cb_ref[...]
        return 0

    lax.fori_loop(0, n_chunks, conv_chunk, 0)

    def sweep(fwd):
        d = 0 if fwd else 1
        neg_lam = -lam_ref[d]
        softplus = jnp.maximum(neg_lam, 0.0) + jnp.log(1.0 + jnp.exp(-jnp.abs(neg_lam)))

        def chunk(step, carry):
            if fwd:
                ci = step
            else:
                ci = jnp.where(step < n_ctx, n_ctx - 1 - step, n_chunks - 1 - (step - n_ctx))
            t0 = pl.multiple_of(ci * TC, TC)
            xc = xc_scr[pl.ds(t0, TC), :]
            xb = xc.astype(BF16)
            r = _sigmoid(jnp.dot(xb, wa_ref[d], preferred_element_type=F32) + ba_ref[d])
            gi = _sigmoid(jnp.dot(xb, wx_ref[d], preferred_element_type=F32) + bx_ref[d])
            log_a = (-RNN_C) * r * softplus
            a = jnp.exp(log_a)
            bb = jnp.sqrt(1.0 - a * a) * (gi * xc)
            a3 = a.reshape(G, SUBLANES, cb)
            b3 = bb.reshape(G, SUBLANES, cb)
            for s in (1, 2, 4):
                shift = s if fwd else SUBLANES - s
                use = (sub >= s) if fwd else (sub < SUBLANES - s)
                a_n, b_n = pltpu.roll(a3, shift, 1), pltpu.roll(b3, shift, 1)
                b3 = jnp.where(use, a3 * b_n + b3, b3)
                a3 = jnp.where(use, a3 * a_n, a3)

            hs = [None] * G
            for k in range(G):
                g = k if fwd else G - 1 - k
                hs[g] = b3[g] + a3[g] * carry
                carry = hs[g][SUBLANES - 1:SUBLANES] if fwd else hs[g][0:1]
            h = jnp.concatenate(hs, axis=0)
            if fwd:
                h_scr[pl.ds(t0, TC), :] = h
            else:
                o_ref[pl.ds(t0, TC), :] = (h_scr[pl.ds(t0, TC), :] + h).astype(o_ref.dtype)
            return carry

        lax.fori_loop(0, n_chunks, chunk, jnp.zeros((1, cb), F32), unroll=RNN_UNROLL)

    sweep(True)
    sweep(False)


def _rglru(p3, conv_w, conv_b, wa, ba, wx, bx, lam, cl):
    B, T, _ = p3.shape
    D = conv_w.shape[1]
    nb = D // MXU_DIM
    vec = lambda: pl.BlockSpec((2, 1, MXU_DIM), lambda b, c: (0, 0, c))
    mat = lambda: pl.BlockSpec((2, None, MXU_DIM, MXU_DIM), lambda b, c: (0, c, 0, 0))
    return pl.pallas_call(
        functools.partial(_rglru_kernel, T=T, CL=cl, TC=RNN_CHUNK),
        grid=(B, nb),
        in_specs=[pl.BlockSpec((None, T, MXU_DIM), lambda b, c: (b, 0, COL_RX * nb + c)),
                  pl.BlockSpec((RNN_CONV, MXU_DIM), lambda b, c: (0, c)),
                  pl.BlockSpec((1, MXU_DIM), lambda b, c: (0, c)),
                  mat(), vec(), mat(), vec(), vec()],
        out_specs=pl.BlockSpec((None, T, MXU_DIM), lambda b, c: (b, 0, c)),
        out_shape=jax.ShapeDtypeStruct((B, T, D), BF16),
        scratch_shapes=[pltpu.VMEM((T, MXU_DIM), F32), pltpu.VMEM((T, MXU_DIM), F32)],
        compiler_params=_params("parallel", "parallel"),
        name="rglru",
    )(p3, conv_w, conv_b.reshape(1, D), wa, ba.reshape(2, 1, D), wx, bx.reshape(2, 1, D),
      lam.reshape(2, 1, D))


def _attn_kernel(q_ref, k_ref, v_ref, lq_ref, g_ref, o_ref, vt_scr, sa0, sa1, sb0, sb1, *, T, CL, lam_init, do_ctx, kc):
    TQ = ATTN_TQ
    n_lat = (T - CL) // TQ
    for c in range(T // TQ):
        vt_scr[:, c * TQ:(c + 1) * TQ] = v_ref[c * TQ:(c + 1) * TQ, :].T

    lq = lq_ref[...]
    lam = (jnp.exp(jnp.sum(lq[0:1] * lq[1:2], axis=1, keepdims=True))
           - jnp.exp(jnp.sum(lq[2:3] * lq[3:4], axis=1, keepdims=True)) + lam_init)

    def step(key_rows, fin, nxt):
        if nxt is not None:
            r_n, s_n = nxt
            qt = (q_ref[pl.ds(r_n, TQ), :].astype(F32) * (HEAD_DIM ** -0.5 * math.log2(math.e))).T
            row = lax.broadcasted_iota(jnp.int32, qt.shape, 0)
            map0 = (row % HEAD_DIM) < (HEAD_DIM // 2)
            qts = (jnp.where(map0, qt, 0.0).astype(BF16), jnp.where(map0, 0.0, qt).astype(BF16))
            m_n = [jnp.full((1, TQ), -1e30, F32)] * 2
        if fin is not None:
            r_f, s_f, m_f = fin
            l_f = [jnp.zeros((1, TQ), F32)] * 2
            acc = [jnp.zeros((VALUE_DIM, TQ), F32)] * 2
        for k0 in range(0, key_rows, kc):
            size = min(kc, key_rows - k0)
            if fin is not None:
                vt = vt_scr[:, k0:k0 + size]
                for i in range(2):
                    p = jnp.exp2(s_f[i][k0:k0 + size, :] - m_f[i])
                    l_f[i] = l_f[i] + jnp.sum(p, axis=0, keepdims=True)
                    acc[i] = acc[i] + jnp.dot(vt, p.astype(BF16), preferred_element_type=F32)
            if nxt is not None:
                kt = k_ref[k0:k0 + size, :]
                for i in range(2):
                    s = jnp.dot(kt, qts[i], preferred_element_type=F32)
                    s_n[i][k0:k0 + size, :] = s
                    m_n[i] = jnp.maximum(m_n[i], jnp.max(s, axis=0, keepdims=True))
        if fin is not None:
            o = acc[0] / l_f[0] - lam * (acc[1] / l_f[1])
            y = o * lax.rsqrt(jnp.mean(o * o, axis=0, keepdims=True) + NORM_EPS) * g_ref[...]
            o_ref[pl.ds(r_f, TQ), :] = (y * (1.0 - lam_init)).T.astype(o_ref.dtype)
        return tuple(m_n) if nxt is not None else None

    buf_a, buf_b = (sa0, sa1), (sb0, sb1)
    if do_ctx:
        for t in range(CL // TQ):
            m = step(CL, None, (t * TQ, buf_a))
            step(CL, (t * TQ, buf_a, m), None)
    else:
        o_ref[0:CL, :] = jnp.zeros((CL, VALUE_DIM), o_ref.dtype)

    def pair(jj, ma):
        r = pl.multiple_of(CL + 2 * jj * TQ, TQ)
        mb = step(T, (r, buf_a, ma), (r + TQ, buf_b))
        return step(T, (r + TQ, buf_b, mb), (r + 2 * TQ, buf_a))

    ma = lax.fori_loop(0, n_lat // 2 - 1, pair, step(T, None, (CL, buf_a)))
    r = T - 2 * TQ
    mb = step(T, (r, buf_a, ma), (r + TQ, buf_b))
    step(T, (r + TQ, buf_b, mb), None)


def _attention(p3, lq, g_subln, cl, lam_init, do_ctx, kc):
    B, T, _ = p3.shape
    assert kc % LANES == 0 and cl % LANES == 0 and T % LANES == 0
    D = ATTN_HEADS * VALUE_DIM
    cpb = D // VALUE_DIM
    seq = lambda col: pl.BlockSpec((None, T, VALUE_DIM), lambda b, h: (b, 0, col * cpb + h))
    return pl.pallas_call(
        functools.partial(_attn_kernel, T=T, CL=cl, lam_init=lam_init, do_ctx=do_ctx, kc=kc),
        grid=(B, ATTN_HEADS),
        in_specs=[seq(COL_Q), seq(COL_K), seq(COL_V),
                  pl.BlockSpec((4, HEAD_DIM), lambda b, h: (0, 0)),
                  pl.BlockSpec((VALUE_DIM, 1), lambda b, h: (0, 0))],
        out_specs=pl.BlockSpec((None, T, VALUE_DIM), lambda b, h: (b, 0, h)),
        out_shape=jax.ShapeDtypeStruct((B, T, D), BF16),
        scratch_shapes=[pltpu.VMEM((VALUE_DIM, T), BF16)] + [pltpu.VMEM((T, ATTN_TQ), F32)] * 4,
        compiler_params=_params("parallel", "parallel"),
        name="diff_attention",
    )(p3, p3, p3, lq, g_subln.reshape(VALUE_DIM, 1))


def _conv_kernel(v_ref, vp_ref, vn_ref, g_ref, gp_ref, gn_ref, w_ref, b_ref, lg_ref, lb_ref, o_ref,
                 *, T, CL, TC):
    i = pl.program_id(1)
    n_ctx, n_chunks = CL // TC, T // TC
    seg_first = jnp.logical_or(i == 0, i == n_ctx)
    seg_last = jnp.logical_or(i == n_ctx - 1, i == n_chunks - 1)

    def gated(v, g):
        return v[...].astype(F32) * _sigmoid(g[...].astype(F32))

    ext = jnp.concatenate([gated(vp_ref, gp_ref) * jnp.where(seg_first, 0.0, 1.0),
                           gated(v_ref, g_ref),
                           gated(vn_ref, gn_ref) * jnp.where(seg_last, 0.0, 1.0)], axis=0)
    w = w_ref[...]
    cols = []
    for c in range(ext.shape[1] // LANES):
        sl = slice(c * LANES, (c + 1) * LANES)
        taps = [(k - (CONV_K - 1) // 2, w[k:k + 1, sl]) for k in range(CONV_K)]
        cols.append(_conv_rows(ext[:, sl], taps, TC))
    z = jnp.concatenate(cols, axis=1) + b_ref[...]
    mu = jnp.mean(z, axis=-1, keepdims=True)
    zc = z - mu
    var = jnp.mean(zc * zc, axis=-1, keepdims=True)
    y = zc * lax.rsqrt(var + NORM_EPS) * lg_ref[...] + lb_ref[...]
    o_ref[...] = _silu(y).astype(o_ref.dtype)


def _conformer_conv(p3, dw_w, dw_b, ln_g, ln_b, cl):
    B, T, _ = p3.shape
    D = dw_w.shape[1]
    TC = ROW_TILE
    hb = TC // HALO
    last = T // HALO - 1

    def main(col):
        return pl.BlockSpec((None, TC, D), lambda b, i: (b, i, col))

    def prev(col):
        return pl.BlockSpec((None, HALO, D), lambda b, i: (b, jnp.maximum(i * hb - 1, 0), col))

    def nxt(col):
        return pl.BlockSpec((None, HALO, D), lambda b, i: (b, jnp.minimum((i + 1) * hb, last), col))

    vec = lambda: pl.BlockSpec((1, D), lambda b, i: (0, 0))
    return pl.pallas_call(
        functools.partial(_conv_kernel, T=T, CL=cl, TC=TC),
        grid=(B, T // TC),
        in_specs=[main(COL_CV), prev(COL_CV), nxt(COL_CV), main(COL_CG), prev(COL_CG), nxt(COL_CG),
                  pl.BlockSpec((CONV_K, D), lambda b, i: (0, 0)), vec(), vec(), vec()],
        out_specs=pl.BlockSpec((None, TC, D), lambda b, i: (b, i, 0)),
        out_shape=jax.ShapeDtypeStruct((B, T, D), BF16),
        compiler_params=_params("parallel", "parallel"),
        name="conformer_conv",
    )(p3, p3, p3, p3, p3, p3, dw_w, dw_b.reshape(1, D), ln_g.reshape(1, D), ln_b.reshape(1, D))


def _merge_kernel(hr_ref, rg_ref, oa_ref, zc_ref, g0_ref, g1_ref, g2_ref, h_ref, m_ref,
                  wr_ref, wa_ref, wc_ref, wo_ref, o_ref):
    f32 = lambda r: r[...].astype(F32)
    rec = f32(hr_ref) * _gelu_tanh(f32(rg_ref))
    y_r = jnp.dot(rec.astype(BF16), wr_ref[...], preferred_element_type=F32)
    y_a = jnp.dot(oa_ref[...], wa_ref[...], preferred_element_type=F32)
    y_c = jnp.dot(zc_ref[...], wc_ref[...], preferred_element_type=F32)
    mix = _sigmoid(f32(g0_ref)) * y_r + _sigmoid(f32(g1_ref)) * y_a + _sigmoid(f32(g2_ref)) * y_c
    upd = jnp.dot(mix.astype(BF16), wo_ref[...], preferred_element_type=F32)
    o_ref[...] = h_ref[...] + m_ref[...][2:3] * upd


def _merge(hr, p3, oa, zc, h, modtab, w_r, w_a, w_c, w_o, cl, first_tile):
    B, T, D = h.shape
    seg_of = _seg_index(cl // ROW_TILE)
    seg = lambda i: seg_of(i + first_tile)
    tile = lambda col: pl.BlockSpec((None, ROW_TILE, D), lambda b, i: (b, i + first_tile, col))
    wspec = lambda: pl.BlockSpec((D, D), lambda b, i: (0, 0))
    return pl.pallas_call(
        _merge_kernel,
        grid=(B, T // ROW_TILE - first_tile),
        in_specs=[tile(0), tile(COL_RG), tile(0), tile(0), tile(COL_G), tile(COL_G + 1), tile(COL_G + 2),
                  tile(0),
                  pl.BlockSpec((None, None, 8, D), lambda b, i: (b, seg(i), 0, 0)),
                  wspec(), wspec(), wspec(), wspec()],
        out_specs=tile(0),
        out_shape=jax.ShapeDtypeStruct((B, T, D), F32),
        input_output_aliases={7: 0},
        compiler_params=_params("parallel", "parallel"),
        name="merge",
    )(hr, p3, oa, zc, p3, p3, p3, h, modtab, w_r, w_a, w_c, w_o)


def _ffn_update(h_ref, g_ref, m_ref, wi_ref, wo_ref, d_ff):
    h = h_ref[...]
    m = m_ref[...]
    u = _mod_norm(h, g_ref[...], m[3:4], m[4:5]).astype(BF16)
    gu = jnp.dot(u, wi_ref[...], preferred_element_type=F32)
    act = (_silu(gu[:, :d_ff]) * gu[:, d_ff:]).astype(BF16)
    return h + m[5:6] * jnp.dot(act, wo_ref[...], preferred_element_type=F32)


def _ffn_kernel(h_ref, g_ref, m_ref, wi_ref, wo_ref, o_ref, *, d_ff):
    o_ref[...] = _ffn_update(h_ref, g_ref, m_ref, wi_ref, wo_ref, d_ff)


def _ffn_final_kernel(h_ref, g_ref, m_ref, wi_ref, wo_ref, gf_ref, o_ref, *, d_ff):
    x = _ffn_update(h_ref, g_ref, m_ref, wi_ref, wo_ref, d_ff)
    o_ref[...] = x * lax.rsqrt(jnp.mean(x * x, axis=-1, keepdims=True) + NORM_EPS) * gf_ref[...]


def _ffn(h, g, modtab, w_i, w_o, cl, first_tile, g_final=None):
    B, T, D = h.shape
    d_ff = w_o.shape[0]
    seg_of = _seg_index(cl // ROW_TILE)
    seg = lambda i: seg_of(i + first_tile)
    tile = pl.BlockSpec((None, ROW_TILE, D), lambda b, i: (b, i + first_tile, 0))
    vec = pl.BlockSpec((1, D), lambda b, i: (0, 0))
    in_specs = [tile, vec,
                pl.BlockSpec((None, None, 8, D), lambda b, i: (b, seg(i), 0, 0)),
                pl.BlockSpec((D, 2 * d_ff), lambda b, i: (0, 0)),
                pl.BlockSpec((d_ff, D), lambda b, i: (0, 0))]
    grid = (B, T // ROW_TILE - first_tile)
    if g_final is None:
        return pl.pallas_call(
            functools.partial(_ffn_kernel, d_ff=d_ff), grid=grid, in_specs=in_specs, out_specs=tile,
            out_shape=jax.ShapeDtypeStruct((B, T, D), F32), input_output_aliases={0: 0},
            compiler_params=_params("parallel", "parallel"), name="ffn",
        )(h, g.reshape(1, D), modtab, w_i, w_o)
    assert first_tile * ROW_TILE == cl
    return pl.pallas_call(
        functools.partial(_ffn_final_kernel, d_ff=d_ff), grid=grid, in_specs=in_specs + [vec],
        out_specs=pl.BlockSpec((None, ROW_TILE, D), lambda b, i: (b, i, 0)),
        out_shape=jax.ShapeDtypeStruct((B, T - cl, D), F32),
        compiler_params=_params("parallel", "parallel"), name="ffn_final",
    )(h, g.reshape(1, D), modtab, w_i, w_o, g_final.reshape(1, D))


def _rope_tables(cl, s):
    pairs_axis = HEAD_DIM // 4
    rows = jnp.repeat(jnp.arange(s // GRID_W, dtype=F32), GRID_W)
    cols = jnp.tile(jnp.arange(GRID_W, dtype=F32), s // GRID_W)
    inv = ROPE_BASE ** (-jnp.arange(pairs_axis, dtype=F32) / pairs_axis)
    ang = jnp.concatenate([rows[:, None] * inv, cols[:, None] * inv], axis=-1)
    cos = jnp.concatenate([jnp.ones((cl, HEAD_DIM // 2), F32), jnp.cos(ang)], axis=0)
    sin = jnp.concatenate([jnp.zeros((cl, HEAD_DIM // 2), F32), jnp.sin(ang)], axis=0)
    return (jnp.concatenate([cos, cos, cos, cos], axis=1),
            jnp.concatenate([-sin, -sin, sin, sin], axis=1))


def _rope_layout(a):
    lead = a.shape[:-1]
    a = a.reshape(lead + (ATTN_HEADS, 2, 2, HEAD_DIM // 2))
    return jnp.swapaxes(a, -3, -2).reshape(lead + (ATTN_HEADS * VALUE_DIM,))


def _permute_qk_columns(a):
    d = a.shape[-1] // N_COL_BLOCKS
    blocks = [a[..., i * d:(i + 1) * d] for i in range(N_COL_BLOCKS)]
    for i in (COL_K, COL_Q):
        blocks[i] = _rope_layout(blocks[i])
    return jnp.concatenate(blocks, axis=-1)


def _blockdiag_tiles(w):
    two, nb, bs, _ = w.shape
    per = MXU_DIM // bs
    w = w.reshape(two, nb // per, per, bs, bs)
    eye = jnp.eye(per, dtype=w.dtype)
    t = jnp.einsum('dtpio,pq->dtpiqo', w, eye)
    return t.reshape(two, nb // per, MXU_DIM, MXU_DIM)


def kernel(x, c, ctx, c_ctx, w_mod, b_mod, g_norm1, g_norm2, w_in, b_in, rnn_conv_w, rnn_conv_b, rnn_w_a, rnn_b_a, rnn_w_x, rnn_b_x, rnn_lambda, w_rnn_o, lambda_qk, g_subln, w_attn_o, conv_dw_w, conv_dw_b, conv_ln_g, conv_ln_b, w_conv_o, w_out, w_ffn_in, w_ffn_out, g_final):
    B, S, D = x.shape
    CL = ctx.shape[1]
    T = CL + S
    L = w_mod.shape[0]
    assert D == ATTN_HEADS * VALUE_DIM and S % GRID_W == 0
    assert CL % ROW_TILE == 0 and S % ROW_TILE == 0 and ROW_TILE == ATTN_TQ
    assert (S // ATTN_TQ) % 2 == 0

    h = jnp.concatenate([ctx, x], axis=1)

    n_cond = -(-(B + 1) // SUBLANES) * SUBLANES
    cc = jnp.zeros((n_cond, D), F32).at[:B].set(c).at[B].set(c_ctx)
    mod = _modulation(cc, w_mod, b_mod).reshape(L, n_cond, N_MOD, D)
    mod_ctx = jnp.broadcast_to(mod[:, B][:, None], (L, B, N_MOD, D))
    modtab = jnp.stack([mod_ctx, mod[:, :B]], axis=2)
    modtab = jnp.pad(modtab, ((0, 0), (0, 0), (0, 0), (0, 8 - N_MOD), (0, 0)))

    cos_t, sin_t = _rope_tables(CL, S)
    w_in_b = _permute_qk_columns(w_in).astype(BF16)
    b_in_p = _permute_qk_columns(b_in)

    for l in range(L):
        lam_init = 0.8 - 0.6 * math.exp(-0.3 * l)
        first_tile = CL // ROW_TILE if l == L - 1 else 0
        p = _in_proj(h.reshape(B * T, D), g_norm1[l], modtab[l], w_in_b[l], b_in_p[l], cos_t, sin_t, CL,
                     T // PROJ_TILES, PROJ_SPLIT)
        p3 = p.reshape(B, T, N_COL_BLOCKS * D)
        hr = _rglru(p3, rnn_conv_w[l], rnn_conv_b[l],
                    _blockdiag_tiles(rnn_w_a[l]).astype(BF16), rnn_b_a[l],
                    _blockdiag_tiles(rnn_w_x[l]).astype(BF16), rnn_b_x[l], rnn_lambda[l], CL)
        oa = _attention(p3, lambda_qk[l], g_subln[l], CL, lam_init, first_tile == 0, ATTN_KC)
        zc = _conformer_conv(p3, conv_dw_w[l], conv_dw_b[l], conv_ln_g[l], conv_ln_b[l], CL)
        h = _merge(hr, p3, oa, zc, h, modtab[l], w_rnn_o[l].astype(BF16), w_attn_o[l].astype(BF16),
                   w_conv_o[l].astype(BF16), w_out[l].astype(BF16), CL, first_tile)
        h = _ffn(h, g_norm2[l], modtab[l], w_ffn_in[l].astype(BF16), w_ffn_out[l].astype(BF16), CL, first_tile,
                 g_final if l == L - 1 else None)
    return h
```

```python
import functools
import math

import jax
import jax.numpy as jnp
from jax import lax
from jax.experimental import pallas as pl
from jax.experimental.pallas import tpu as pltpu

F32 = jnp.float32
BF16 = jnp.bfloat16

NORM_EPS = 1e-6
N_MOD = 6
ATTN_HEADS = 8
HEAD_DIM = 64
VALUE_DIM = 2 * HEAD_DIM
GRID_W = 64
ROPE_BASE = 10000.0
RNN_BLOCK = 64
RNN_CONV = 4
RNN_C = 8.0
CONV_K = 31
LANES = 128
SUBLANES = 8
BF16_ROWS = 16
HALO = 16
MXU_DIM = 256
VMEM_LIMIT = 56 * 1024 * 1024

COL_RX, COL_K, COL_V, COL_RG, COL_Q, COL_CV, COL_CG, COL_G = 0, 1, 2, 3, 4, 5, 6, 7
N_COL_BLOCKS = 10

ROW_TILE = 256
RNN_CHUNK = 256
RNN_UNROLL = 4
ATTN_TQ = 256
ATTN_KC = 256
PROJ_TILES = 2
PROJ_SPLIT = 4


def _sigmoid(v):
    return 1.0 / (1.0 + jnp.exp(-v))


def _silu(v):
    return v * _sigmoid(v)


def _gelu_tanh(v):
    return 0.5 * v * (1.0 + jnp.tanh(math.sqrt(2.0 / math.pi) * (v + 0.044715 * (v * v * v))))


def _params(*sem):
    return pltpu.CompilerParams(dimension_semantics=sem, vmem_limit_bytes=VMEM_LIMIT)


def _mod_kernel(c_ref, w_ref, b_ref, o_ref):
    s = _silu(c_ref[...])
    o_ref[...] = jnp.dot(s, w_ref[...], preferred_element_type=F32,
                         precision=lax.Precision.HIGHEST) + b_ref[...]


def _modulation(cc, w_mod, b_mod):
    L, D, _ = w_mod.shape
    R = cc.shape[0]
    return pl.pallas_call(
        _mod_kernel,
        grid=(L, N_MOD),
        in_specs=[pl.BlockSpec((R, D), lambda l, j: (0, 0)),
                  pl.BlockSpec((None, D, D), lambda l, j: (l, 0, j)),
                  pl.BlockSpec((None, 1, D), lambda l, j: (l, 0, j))],
        out_specs=pl.BlockSpec((None, R, D), lambda l, j: (l, 0, j)),
        out_shape=jax.ShapeDtypeStruct((L, R, N_MOD * D), F32),
        compiler_params=_params("arbitrary", "arbitrary"),
        name="modulation",
    )(cc, w_mod, b_mod.reshape(L, 1, N_MOD * D))


def _seg_index(n_ctx_tiles):
    return lambda i: jnp.where(i >= n_ctx_tiles, 1, 0)


def _mod_norm(x, g, shift, scale):
    y = x * lax.rsqrt(jnp.mean(x * x, axis=-1, keepdims=True) + NORM_EPS) * g
    return y * (1.0 + scale) + shift


def _rope(t, cos, sin_signed):
    outs = []
    for k in range(t.shape[1] // LANES):
        tk = t[:, k * LANES:(k + 1) * LANES]
        outs.append(tk * cos + pltpu.roll(tk, LANES // 2, 1) * sin_signed)
    return jnp.concatenate(outs, axis=1)


def _proj_kernel(h_ref, g_ref, m_ref, w_ref, b_ref, cos_ref, sin_ref, o_ref, u_scr, *, CL, tiles_per_seq, split):
    i, j = pl.program_id(0), pl.program_id(1)
    tm = h_ref.shape[0]

    @pl.when(j == 0)
    def _():
        m = m_ref[...]
        row = (i % tiles_per_seq) * tm + lax.broadcasted_iota(jnp.int32, (tm, 1), 0)
        is_ctx = row < CL
        shift = jnp.where(is_ctx, m[0, 0:1], m[1, 0:1])
        scale = jnp.where(is_ctx, m[0, 1:2], m[1, 1:2])
        u_scr[...] = _mod_norm(h_ref[...], g_ref[...], shift, scale).astype(u_scr.dtype)

    is_rope = jnp.logical_or(j == COL_K, j == COL_Q)
    rb = tm // split

    def project(k):
        sl = slice(k * rb, (k + 1) * rb)
        return sl, jnp.dot(u_scr[sl, :], w_ref[...], preferred_element_type=F32) + b_ref[...]

    @pl.when(is_rope)
    def _():
        for k in range(split):
            sl, acc = project(k)
            o_ref[sl, :] = _rope(acc, cos_ref[sl, :], sin_ref[sl, :]).astype(o_ref.dtype)

    @pl.when(jnp.logical_not(is_rope))
    def _():
        for k in range(split):
            sl, acc = project(k)
            o_ref[sl, :] = acc.astype(o_ref.dtype)


def _in_proj(h2, g, modtab, w, b, cos_t, sin_t, cl, tm, split):
    M, D = h2.shape
    N = w.shape[1]
    T = cos_t.shape[0]
    tiles_per_seq = T // tm
    assert T % tm == 0 and tm % (split * BF16_ROWS) == 0
    return pl.pallas_call(
        functools.partial(_proj_kernel, CL=cl, tiles_per_seq=tiles_per_seq, split=split),
        grid=(M // tm, N // D),
        in_specs=[pl.BlockSpec((tm, D), lambda i, j: (i, 0)),
                  pl.BlockSpec((1, D), lambda i, j: (0, 0)),
                  pl.BlockSpec((None, 2, 8, D), lambda i, j: (i // tiles_per_seq, 0, 0, 0)),
                  pl.BlockSpec((D, D), lambda i, j: (0, j)),
                  pl.BlockSpec((1, D), lambda i, j: (0, j)),
                  pl.BlockSpec((tm, LANES), lambda i, j: (i % tiles_per_seq, 0)),
                  pl.BlockSpec((tm, LANES), lambda i, j: (i % tiles_per_seq, 0))],
        out_specs=pl.BlockSpec((tm, D), lambda i, j: (i, j)),
        out_shape=jax.ShapeDtypeStruct((M, N), BF16),
        scratch_shapes=[pltpu.VMEM((tm, D), BF16)],
        compiler_params=_params("parallel", "arbitrary"),
        name="in_proj",
    )(h2, g.reshape(1, D), modtab, w, b.reshape(1, N), cos_t, sin_t)


def _conv_rows(ext, taps, rows_out):
    n = ext.shape[0]
    rolled = {}
    acc = None
    for off, w in taps:
        start = HALO + off
        r, q = start % SUBLANES, start // SUBLANES
        if r not in rolled:
            rolled[r] = ext if r == 0 else pltpu.roll(ext, n - r, 0)
        term = w * rolled[r][q * SUBLANES:q * SUBLANES + rows_out]
        acc = term if acc is None else acc + term
    return acc


def _rglru_kernel(x_ref, cw_ref, cb_ref, wa_ref, ba_ref, wx_ref, bx_ref, lam_ref, o_ref,
                  xc_scr, h_scr, *, T, CL, TC):
    n_chunks, n_ctx = T // TC, CL // TC
    G = TC // SUBLANES
    cb = x_ref.shape[1]
    cw = cw_ref[...]
    taps = [(k - 2, cw[k:k + 1]) for k in range(RNN_CONV)]
    sub = lax.broadcasted_iota(jnp.int32, (G, SUBLANES, cb), 1)

    def conv_chunk(ci, _):
        t0 = pl.multiple_of(ci * TC, TC)
        seg_first = jnp.logical_or(ci == 0, ci == n_ctx)
        seg_last = jnp.logical_or(ci == n_ctx - 1, ci == n_chunks - 1)
        p0 = pl.multiple_of(jnp.maximum(t0 - HALO, 0), HALO)
        n0 = pl.multiple_of(jnp.minimum(t0 + TC, T - HALO), HALO)
        prev = x_ref[pl.ds(p0, HALO), :].astype(F32) * jnp.where(seg_first, 0.0, 1.0)
        nxt = x_ref[pl.ds(n0, HALO), :].astype(F32) * jnp.where(seg_last, 0.0, 1.0)
        main = x_ref[pl.ds(t0, TC), :].astype(F32)
        xc_scr[pl.ds(t0, TC), :] = _conv_rows(jnp.concatenate([prev, main, nxt], axis=0), taps, TC) + cb_ref[...]
        return 0

    lax.fori_loop(0, n_chunks, conv_chunk, 0)

    def sweep(fwd):
        d = 0 if fwd else 1
        neg_lam = -lam_ref[d]
        softplus = jnp.maximum(neg_lam, 0.0) + jnp.log(1.0 + jnp.exp(-jnp.abs(neg_lam)))

        def chunk(step, carry):
            if fwd:
                ci = step
            else:
                ci = jnp.where(step < n_ctx, n_ctx - 1 - step, n_chunks - 1 - (step - n_ctx))
            t0 = pl.multiple_of(ci * TC, TC)
            xc = xc_scr[pl.ds(t0, TC), :]
            xb = xc.astype(BF16)
            r = _sigmoid(jnp.dot(xb, wa_ref[d], preferred_element_type=F32) + ba_ref[d])
            gi = _sigmoid(jnp.dot(xb, wx_ref[d], preferred_element_type=F32) + bx_ref[d])
            log_a = (-RNN_C) * r * softplus
            a = jnp.exp(log_a)
            bb = jnp.sqrt(1.0 - a * a) * (gi * xc)
            a3 = a.reshape(G, SUBLANES, cb)
            b3 = bb.reshape(G, SUBLANES, cb)
            for s in (1, 2, 4):
                shift = s if fwd else SUBLANES - s
                use = (sub >= s) if fwd else (sub < SUBLANES - s)
                a_n, b_n = pltpu.roll(a3, shift, 1), pltpu.roll(b3, shift, 1)
                b3 = jnp.where(use, a3 * b_n + b3, b3)
                a3 = jnp.where(use, a3 * a_n, a3)

            hs = [None] * G
            for k in range(G):
                g = k if fwd else G - 1 - k
                hs[g] = b3[g] + a3[g] * carry
                carry = hs[g][SUBLANES - 1:SUBLANES] if fwd else hs[g][0:1]
            h = jnp.concatenate(hs, axis=0)
            if fwd:
                h_scr[pl.ds(t0, TC), :] = h
            else:
                o_ref[pl.ds(t0, TC), :] = (h_scr[pl.ds(t0, TC), :] + h).astype(o_ref.dtype)
            return carry

        lax.fori_loop(0, n_chunks, chunk, jnp.zeros((1, cb), F32), unroll=RNN_UNROLL)

    sweep(True)
    sweep(False)


def _rglru(p3, conv_w, conv_b, wa, ba, wx, bx, lam, cl):
    B, T, _ = p3.shape
    D = conv_w.shape[1]
    nb = D // MXU_DIM
    vec = lambda: pl.BlockSpec((2, 1, MXU_DIM), lambda b, c: (0, 0, c))
    mat = lambda: pl.BlockSpec((2, None, MXU_DIM, MXU_DIM), lambda b, c: (0, c, 0, 0))
    return pl.pallas_call(
        functools.partial(_rglru_kernel, T=T, CL=cl, TC=RNN_CHUNK),
        grid=(B, nb),
        in_specs=[pl.BlockSpec((None, T, MXU_DIM), lambda b, c: (b, 0, COL_RX * nb + c)),
                  pl.BlockSpec((RNN_CONV, MXU_DIM), lambda b, c: (0, c)),
                  pl.BlockSpec((1, MXU_DIM), lambda b, c: (0, c)),
                  mat(), vec(), mat(), vec(), vec()],
        out_specs=pl.BlockSpec((None, T, MXU_DIM), lambda b, c: (b, 0, c)),
        out_shape=jax.ShapeDtypeStruct((B, T, D), BF16),
        scratch_shapes=[pltpu.VMEM((T, MXU_DIM), F32), pltpu.VMEM((T, MXU_DIM), F32)],
        compiler_params=_params("parallel", "parallel"),
        name="rglru",
    )(p3, conv_w, conv_b.reshape(1, D), wa, ba.reshape(2, 1, D), wx, bx.reshape(2, 1, D),
      lam.reshape(2, 1, D))


def _attn_kernel(q_ref, k_ref, v_ref, lq_ref, g_ref, o_ref, vt_scr, sa0, sa1, sb0, sb1, *, T, CL, lam_init, do_ctx, kc):
    TQ = ATTN_TQ
    n_lat = (T - CL) // TQ
    for c in range(T // TQ):
        vt_scr[:, c * TQ:(c + 1) * TQ] = v_ref[c * TQ:(c + 1) * TQ, :].T

    lq = lq_ref[...]
    lam = (jnp.exp(jnp.sum(lq[0:1] * lq[1:2], axis=1, keepdims=True))
           - jnp.exp(jnp.sum(lq[2:3] * lq[3:4], axis=1, keepdims=True)) + lam_init)

    def step(key_rows, fin, nxt):
        if nxt is not None:
            r_n, s_n = nxt
            qt = (q_ref[pl.ds(r_n, TQ), :].astype(F32) * (HEAD_DIM ** -0.5 * math.log2(math.e))).T
            row = lax.broadcasted_iota(jnp.int32, qt.shape, 0)
            map0 = (row % HEAD_DIM) < (HEAD_DIM // 2)
            qts = (jnp.where(map0, qt, 0.0).astype(BF16), jnp.where(map0, 0.0, qt).astype(BF16))
            m_n = [jnp.full((1, TQ), -1e30, F32)] * 2
        if fin is not None:
            r_f, s_f, m_f = fin
            l_f = [jnp.zeros((1, TQ), F32)] * 2
            acc = [jnp.zeros((VALUE_DIM, TQ), F32)] * 2
        for k0 in range(0, key_rows, kc):
            size = min(kc, key_rows - k0)
            if fin is not None:
                vt = vt_scr[:, k0:k0 + size]
                for i in range(2):
                    p = jnp.exp2(s_f[i][k0:k0 + size, :] - m_f[i])
                    l_f[i] = l_f[i] + jnp.sum(p, axis=0, keepdims=True)
                    acc[i] = acc[i] + jnp.dot(vt, p.astype(BF16), preferred_element_type=F32)
            if nxt is not None:
                kt = k_ref[k0:k0 + size, :]
                for i in range(2):
                    s = jnp.dot(kt, qts[i], preferred_element_type=F32)
                    s_n[i][k0:k0 + size, :] = s
                    m_n[i] = jnp.maximum(m_n[i], jnp.max(s, axis=0, keepdims=True))
        if fin is not None:
            o = acc[0] / l_f[0] - lam * (acc[1] / l_f[1])
            y = o * lax.rsqrt(jnp.mean(o * o, axis=0, keepdims=True) + NORM_EPS) * g_ref[...]
            o_ref[pl.ds(r_f, TQ), :] = (y * (1.0 - lam_init)).T.astype(o_ref.dtype)
        return tuple(m_n) if nxt is not None else None

    buf_a, buf_b = (sa0, sa1), (sb0, sb1)
    if do_ctx:
        for t in range(CL // TQ):
            m = step(CL, None, (t * TQ, buf_a))
            step(CL, (t * TQ, buf_a, m), None)
    else:
        o_ref[0:CL, :] = jnp.zeros((CL, VALUE_DIM), o_ref.dtype)

    def pair(jj, ma):
        r = pl.multiple_of(CL + 2 * jj * TQ, TQ)
        mb = step(T, (r, buf_a, ma), (r + TQ, buf_b))
        return step(T, (r + TQ, buf_b, mb), (r + 2 * TQ, buf_a))

    ma = lax.fori_loop(0, n_lat // 2 - 1, pair, step(T, None, (CL, buf_a)))
    r = T - 2 * TQ
    mb = step(T, (r, buf_a, ma), (r + TQ, buf_b))
    step(T, (r + TQ, buf_b, mb), None)


def _attention(p3, lq, g_subln, cl, lam_init, do_ctx, kc):
    B, T, _ = p3.shape
    assert kc % LANES == 0 and cl % LANES == 0 and T % LANES == 0
    D = ATTN_HEADS * VALUE_DIM
    cpb = D // VALUE_DIM
    seq = lambda col: pl.BlockSpec((None, T, VALUE_DIM), lambda b, h: (b, 0, col * cpb + h))
    return pl.pallas_call(
        functools.partial(_attn_kernel, T=T, CL=cl, lam_init=lam_init, do_ctx=do_ctx, kc=kc),
        grid=(B, ATTN_HEADS),
        in_specs=[seq(COL_Q), seq(COL_K), seq(COL_V),
                  pl.BlockSpec((4, HEAD_DIM), lambda b, h: (0, 0)),
                  pl.BlockSpec((VALUE_DIM, 1), lambda b, h: (0, 0))],
        out_specs=pl.BlockSpec((None, T, VALUE_DIM), lambda b, h: (b, 0, h)),
        out_shape=jax.ShapeDtypeStruct((B, T, D), BF16),
        scratch_shapes=[pltpu.VMEM((VALUE_DIM, T), BF16)] + [pltpu.VMEM((T, ATTN_TQ), F32)] * 4,
        compiler_params=_params("parallel", "parallel"),
        name="diff_attention",
    )(p3, p3, p3, lq, g_subln.reshape(VALUE_DIM, 1))


def _conv_kernel(v_ref, vp_ref, vn_ref, g_ref, gp_ref, gn_ref, w_ref, b_ref, lg_ref, lb_ref, o_ref,
                 *, T, CL, TC):
    i = pl.program_id(1)
    n_ctx, n_chunks = CL // TC, T // TC
    seg_first = jnp.logical_or(i == 0, i == n_ctx)
    seg_last = jnp.logical_or(i == n_ctx - 1, i == n_chunks - 1)

    def gated(v, g):
        return v[...].astype(F32) * _sigmoid(g[...].astype(F32))

    ext = jnp.concatenate([gated(vp_ref, gp_ref) * jnp.where(seg_first, 0.0, 1.0),
                           gated(v_ref, g_ref),
                           gated(vn_ref, gn_ref) * jnp.where(seg_last, 0.0, 1.0)], axis=0)
    w = w_ref[...]
    cols = []
    for c in range(ext.shape[1] // LANES):
        sl = slice(c * LANES, (c + 1) * LANES)
        taps = [(k - (CONV_K - 1) // 2, w[k:k + 1, sl]) for k in range(CONV_K)]
        cols.append(_conv_rows(ext[:, sl], taps, TC))
    z = jnp.concatenate(cols, axis=1) + b_ref[...]
    mu = jnp.mean(z, axis=-1, keepdims=True)
    zc = z - mu
    var = jnp.mean(zc * zc, axis=-1, keepdims=True)
    y = zc * lax.rsqrt(var + NORM_EPS) * lg_ref[...] + lb_ref[...]
    o_ref[...] = _silu(y).astype(o_ref.dtype)


def _conformer_conv(p3, dw_w, dw_b, ln_g, ln_b, cl):
    B, T, _ = p3.shape
    D = dw_w.shape[1]
    TC = ROW_TILE
    hb = TC // HALO
    last = T // HALO - 1

    def main(col):
        return pl.BlockSpec((None, TC, D), lambda b, i: (b, i, col))

    def prev(col):
        return pl.BlockSpec((None, HALO, D), lambda b, i: (b, jnp.maximum(i * hb - 1, 0), col))

    def nxt(col):
        return pl.BlockSpec((None, HALO, D), lambda b, i: (b, jnp.minimum((i + 1) * hb, last), col))

    vec = lambda: pl.BlockSpec((1, D), lambda b, i: (0, 0))
    return pl.pallas_call(
        functools.partial(_conv_kernel, T=T, CL=cl, TC=TC),
        grid=(B, T // TC),
        in_specs=[main(COL_CV), prev(COL_CV), nxt(COL_CV), main(COL_CG), prev(COL_CG), nxt(COL_CG),
                  pl.BlockSpec((CONV_K, D), lambda b, i: (0, 0)), vec(), vec(), vec()],
        out_specs=pl.BlockSpec((None, TC, D), lambda b, i: (b, i, 0)),
        out_shape=jax.ShapeDtypeStruct((B, T, D), BF16),
        compiler_params=_params("parallel", "parallel"),
        name="conformer_conv",
    )(p3, p3, p3, p3, p3, p3, dw_w, dw_b.reshape(1, D), ln_g.reshape(1, D), ln_b.reshape(1, D))


def _token_kernel(hr_ref, rg_ref, oa_ref, zc_ref, g0_ref, g1_ref, g2_ref, h_ref, m_ref, wr_ref, wa_ref, wc_ref,
                  wo_ref, gn_ref, wi_ref, wf_ref, *rest, d_ff, final):
    f32 = lambda r: r[...].astype(F32)
    m = m_ref[...]
    rec = f32(hr_ref) * _gelu_tanh(f32(rg_ref))
    y_r = jnp.dot(rec.astype(BF16), wr_ref[...], preferred_element_type=F32)
    y_a = jnp.dot(oa_ref[...], wa_ref[...], preferred_element_type=F32)
    y_c = jnp.dot(zc_ref[...], wc_ref[...], preferred_element_type=F32)
    mix = _sigmoid(f32(g0_ref)) * y_r + _sigmoid(f32(g1_ref)) * y_a + _sigmoid(f32(g2_ref)) * y_c
    h = h_ref[...] + m[2:3] * jnp.dot(mix.astype(BF16), wo_ref[...], preferred_element_type=F32)
    u = _mod_norm(h, gn_ref[...], m[3:4], m[4:5]).astype(BF16)
    gu = jnp.dot(u, wi_ref[...], preferred_element_type=F32)
    act = (_silu(gu[:, :d_ff]) * gu[:, d_ff:]).astype(BF16)
    h = h + m[5:6] * jnp.dot(act, wf_ref[...], preferred_element_type=F32)
    if final:
        gf_ref, o_ref = rest
        o_ref[...] = h * lax.rsqrt(jnp.mean(h * h, axis=-1, keepdims=True) + NORM_EPS) * gf_ref[...]
    else:
        (o_ref,) = rest
        o_ref[...] = h


def _token_update(hr, p3, oa, zc, h, modtab, w_r, w_a, w_c, w_o, g2, w_i, w_f, cl, first_tile, g_final=None):
    B, T, D = h.shape
    d_ff = w_f.shape[0]
    seg_of = _seg_index(cl // ROW_TILE)
    seg = lambda i: seg_of(i + first_tile)
    tile = lambda col: pl.BlockSpec((None, ROW_TILE, D), lambda b, i: (b, i + first_tile, col))
    once = lambda shape: pl.BlockSpec(shape, lambda b, i: (0, 0), pipeline_mode=pl.Buffered(1))
    in_specs = [tile(0), tile(COL_RG), tile(0), tile(0), tile(COL_G), tile(COL_G + 1), tile(COL_G + 2), tile(0),
                pl.BlockSpec((None, None, 8, D), lambda b, i: (b, seg(i), 0, 0)),
                once((D, D)), once((D, D)), once((D, D)), once((D, D)),
                once((1, D)), once((D, 2 * d_ff)), once((d_ff, D))]
    args = [hr, p3, oa, zc, p3, p3, p3, h, modtab, w_r, w_a, w_c, w_o, g2.reshape(1, D), w_i, w_f]
    grid = (B, T // ROW_TILE - first_tile)
    if g_final is None:
        return pl.pallas_call(
            functools.partial(_token_kernel, d_ff=d_ff, final=False), grid=grid, in_specs=in_specs,
            out_specs=tile(0), out_shape=jax.ShapeDtypeStruct((B, T, D), F32), input_output_aliases={7: 0},
            compiler_params=_params("parallel", "parallel"), name="token_update",
        )(*args)
    assert first_tile * ROW_TILE == cl
    return pl.pallas_call(
        functools.partial(_token_kernel, d_ff=d_ff, final=True), grid=grid, in_specs=in_specs + [once((1, D))],
        out_specs=pl.BlockSpec((None, ROW_TILE, D), lambda b, i: (b, i, 0)),
        out_shape=jax.ShapeDtypeStruct((B, T - cl, D), F32),
        compiler_params=_params("parallel", "parallel"), name="token_update_final",
    )(*args, g_final.reshape(1, D))


def _rope_tables(cl, s):
    pairs_axis = HEAD_DIM // 4
    rows = jnp.repeat(jnp.arange(s // GRID_W, dtype=F32), GRID_W)
    cols = jnp.tile(jnp.arange(GRID_W, dtype=F32), s // GRID_W)
    inv = ROPE_BASE ** (-jnp.arange(pairs_axis, dtype=F32) / pairs_axis)
    ang = jnp.concatenate([rows[:, None] * inv, cols[:, None] * inv], axis=-1)
    cos = jnp.concatenate([jnp.ones((cl, HEAD_DIM // 2), F32), jnp.cos(ang)], axis=0)
    sin = jnp.concatenate([jnp.zeros((cl, HEAD_DIM // 2), F32), jnp.sin(ang)], axis=0)
    return (jnp.concatenate([cos, cos, cos, cos], axis=1),
            jnp.concatenate([-sin, -sin, sin, sin], axis=1))


def _rope_layout(a):
    lead = a.shape[:-1]
    a = a.reshape(lead + (ATTN_HEADS, 2, 2, HEAD_DIM // 2))
    return jnp.swapaxes(a, -3, -2).reshape(lead + (ATTN_HEADS * VALUE_DIM,))


def _permute_qk_columns(a):
    d = a.shape[-1] // N_COL_BLOCKS
    blocks = [a[..., i * d:(i + 1) * d] for i in range(N_COL_BLOCKS)]
    for i in (COL_K, COL_Q):
        blocks[i] = _rope_layout(blocks[i])
    return jnp.concatenate(blocks, axis=-1)


def _blockdiag_tiles(w):
    two, nb, bs, _ = w.shape
    per = MXU_DIM // bs
    w = w.reshape(two, nb // per, per, bs, bs)
    eye = jnp.eye(per, dtype=w.dtype)
    t = jnp.einsum('dtpio,pq->dtpiqo', w, eye)
    return t.reshape(two, nb // per, MXU_DIM, MXU_DIM)


def kernel(x, c, ctx, c_ctx, w_mod, b_mod, g_norm1, g_norm2, w_in, b_in, rnn_conv_w, rnn_conv_b, rnn_w_a, rnn_b_a, rnn_w_x, rnn_b_x, rnn_lambda, w_rnn_o, lambda_qk, g_subln, w_attn_o, conv_dw_w, conv_dw_b, conv_ln_g, conv_ln_b, w_conv_o, w_out, w_ffn_in, w_ffn_out, g_final):
    B, S, D = x.shape
    CL = ctx.shape[1]
    T = CL + S
    L = w_mod.shape[0]
    assert D == ATTN_HEADS * VALUE_DIM and S % GRID_W == 0
    assert CL % ROW_TILE == 0 and S % ROW_TILE == 0 and ROW_TILE == ATTN_TQ
    assert (S // ATTN_TQ) % 2 == 0

    h = jnp.concatenate([ctx, x], axis=1)

    n_cond = -(-(B + 1) // SUBLANES) * SUBLANES
    cc = jnp.zeros((n_cond, D), F32).at[:B].set(c).at[B].set(c_ctx)
    mod = _modulation(cc, w_mod, b_mod).reshape(L, n_cond, N_MOD, D)
    mod_ctx = jnp.broadcast_to(mod[:, B][:, None], (L, B, N_MOD, D))
    modtab = jnp.stack([mod_ctx, mod[:, :B]], axis=2)
    modtab = jnp.pad(modtab, ((0, 0), (0, 0), (0, 0), (0, 8 - N_MOD), (0, 0)))

    cos_t, sin_t = _rope_tables(CL, S)
    w_in_b = _permute_qk_columns(w_in).astype(BF16)
    b_in_p = _permute_qk_columns(b_in)

    for l in range(L):
        lam_init = 0.8 - 0.6 * math.exp(-0.3 * l)
        first_tile = CL // ROW_TILE if l == L - 1 else 0
        p = _in_proj(h.reshape(B * T, D), g_norm1[l], modtab[l], w_in_b[l], b_in_p[l], cos_t, sin_t, CL,
                     T // PROJ_TILES, PROJ_SPLIT)
        p3 = p.reshape(B, T, N_COL_BLOCKS * D)
        hr = _rglru(p3, rnn_conv_w[l], rnn_conv_b[l],
                    _blockdiag_tiles(rnn_w_a[l]).astype(BF16), rnn_b_a[l],
                    _blockdiag_tiles(rnn_w_x[l]).astype(BF16), rnn_b_x[l], rnn_lambda[l], CL)
        oa = _attention(p3, lambda_qk[l], g_subln[l], CL, lam_init, first_tile == 0, ATTN_KC)
        zc = _conformer_conv(p3, conv_dw_w[l], conv_dw_b[l], conv_ln_g[l], conv_ln_b[l], CL)
        h = _token_update(hr, p3, oa, zc, h, modtab[l], w_rnn_o[l].astype(BF16), w_attn_o[l].astype(BF16),
                          w_conv_o[l].astype(BF16), w_out[l].astype(BF16), g_norm2[l], w_ffn_in[l].astype(BF16),
                          w_ffn_out[l].astype(BF16), CL, first_tile, g_final if l == L - 1 else None)
    return h
```

```python
import functools
import math

import jax
import jax.numpy as jnp
from jax import lax
from jax.experimental import pallas as pl
from jax.experimental.pallas import tpu as pltpu

F32 = jnp.float32
BF16 = jnp.bfloat16

NORM_EPS = 1e-6
N_MOD = 6
ATTN_HEADS = 8
HEAD_DIM = 64
VALUE_DIM = 2 * HEAD_DIM
GRID_W = 64
ROPE_BASE = 10000.0
RNN_BLOCK = 64
RNN_CONV = 4
RNN_C = 8.0
CONV_K = 31
LANES = 128
SUBLANES = 8
BF16_ROWS = 16
HALO = 16
MXU_DIM = 256
VMEM_LIMIT = 56 * 1024 * 1024

COL_RX, COL_K, COL_V, COL_RG, COL_Q, COL_CV, COL_CG, COL_G = 0, 1, 2, 3, 4, 5, 6, 7
N_COL_BLOCKS = 10

ROW_TILE = 256
RNN_CHUNK = 256
ATTN_TQ = 256
ATTN_KC = 256
PROJ_TILES = 2
PROJ_SPLIT_PLAN = (4, 2, 4, 4)
ATTN_ORDER_PLAN = (0, 1, 2, 0)
RNN_PLAN = ((4, False), (8, False), (4, True), (8, True))


def _sigmoid(v):
    return 1.0 / (1.0 + jnp.exp(-v))


def _silu(v):
    return v * _sigmoid(v)


def _gelu_tanh(v):
    return 0.5 * v * (1.0 + jnp.tanh(math.sqrt(2.0 / math.pi) * (v + 0.044715 * (v * v * v))))


def _params(*sem):
    return pltpu.CompilerParams(dimension_semantics=sem, vmem_limit_bytes=VMEM_LIMIT)


def _mod_kernel(c_ref, w_ref, b_ref, o_ref):
    s = _silu(c_ref[...])
    o_ref[...] = jnp.dot(s, w_ref[...], preferred_element_type=F32,
                         precision=lax.Precision.HIGHEST) + b_ref[...]


def _modulation(cc, w_mod, b_mod):
    L, D, _ = w_mod.shape
    R = cc.shape[0]
    return pl.pallas_call(
        _mod_kernel,
        grid=(L, N_MOD),
        in_specs=[pl.BlockSpec((R, D), lambda l, j: (0, 0)),
                  pl.BlockSpec((None, D, D), lambda l, j: (l, 0, j)),
                  pl.BlockSpec((None, 1, D), lambda l, j: (l, 0, j))],
        out_specs=pl.BlockSpec((None, R, D), lambda l, j: (l, 0, j)),
        out_shape=jax.ShapeDtypeStruct((L, R, N_MOD * D), F32),
        compiler_params=_params("arbitrary", "arbitrary"),
        name="modulation",
    )(cc, w_mod, b_mod.reshape(L, 1, N_MOD * D))


def _seg_index(n_ctx_tiles):
    return lambda i: jnp.where(i >= n_ctx_tiles, 1, 0)


def _mod_norm(x, g, shift, scale):
    y = x * lax.rsqrt(jnp.mean(x * x, axis=-1, keepdims=True) + NORM_EPS) * g
    return y * (1.0 + scale) + shift


def _rope(t, cos, sin_signed):
    outs = []
    for k in range(t.shape[1] // LANES):
        tk = t[:, k * LANES:(k + 1) * LANES]
        outs.append(tk * cos + pltpu.roll(tk, LANES // 2, 1) * sin_signed)
    return jnp.concatenate(outs, axis=1)


def _proj_kernel(h_ref, g_ref, m_ref, w_ref, b_ref, cos_ref, sin_ref, o_ref, u_scr, *, CL, tiles_per_seq, split):
    i, j = pl.program_id(0), pl.program_id(1)
    tm = h_ref.shape[0]

    @pl.when(j == 0)
    def _():
        m = m_ref[...]
        row = (i % tiles_per_seq) * tm + lax.broadcasted_iota(jnp.int32, (tm, 1), 0)
        is_ctx = row < CL
        shift = jnp.where(is_ctx, m[0, 0:1], m[1, 0:1])
        scale = jnp.where(is_ctx, m[0, 1:2], m[1, 1:2])
        u_scr[...] = _mod_norm(h_ref[...], g_ref[...], shift, scale).astype(u_scr.dtype)

    is_rope = jnp.logical_or(j == COL_K, j == COL_Q)
    rb = tm // split

    def project(k):
        sl = slice(k * rb, (k + 1) * rb)
        return sl, jnp.dot(u_scr[sl, :], w_ref[...], preferred_element_type=F32) + b_ref[...]

    @pl.when(is_rope)
    def _():
        for k in range(split):
            sl, acc = project(k)
            o_ref[sl, :] = _rope(acc, cos_ref[sl, :], sin_ref[sl, :]).astype(o_ref.dtype)

    @pl.when(jnp.logical_not(is_rope))
    def _():
        for k in range(split):
            sl, acc = project(k)
            o_ref[sl, :] = acc.astype(o_ref.dtype)


def _in_proj(h2, g, modtab, w, b, cos_t, sin_t, cl, tm, split):
    M, D = h2.shape
    N = w.shape[1]
    T = cos_t.shape[0]
    tiles_per_seq = T // tm
    assert T % tm == 0 and tm % (split * BF16_ROWS) == 0
    return pl.pallas_call(
        functools.partial(_proj_kernel, CL=cl, tiles_per_seq=tiles_per_seq, split=split),
        grid=(M // tm, N // D),
        in_specs=[pl.BlockSpec((tm, D), lambda i, j: (i, 0)),
                  pl.BlockSpec((1, D), lambda i, j: (0, 0)),
                  pl.BlockSpec((None, 2, 8, D), lambda i, j: (i // tiles_per_seq, 0, 0, 0)),
                  pl.BlockSpec((D, D), lambda i, j: (0, j)),
                  pl.BlockSpec((1, D), lambda i, j: (0, j)),
                  pl.BlockSpec((tm, LANES), lambda i, j: (i % tiles_per_seq, 0)),
                  pl.BlockSpec((tm, LANES), lambda i, j: (i % tiles_per_seq, 0))],
        out_specs=pl.BlockSpec((tm, D), lambda i, j: (i, j)),
        out_shape=jax.ShapeDtypeStruct((M, N), BF16),
        scratch_shapes=[pltpu.VMEM((tm, D), BF16)],
        compiler_params=_params("parallel", "arbitrary"),
        name="in_proj",
    )(h2, g.reshape(1, D), modtab, w, b.reshape(1, N), cos_t, sin_t)


def _conv_rows(ext, taps, rows_out):
    n = ext.shape[0]
    rolled = {}
    acc = None
    for off, w in taps:
        start = HALO + off
        r, q = start % SUBLANES, start // SUBLANES
        if r not in rolled:
            rolled[r] = ext if r == 0 else pltpu.roll(ext, n - r, 0)
        term = w * rolled[r][q * SUBLANES:q * SUBLANES + rows_out]
        acc = term if acc is None else acc + term
    return acc


def _rglru_kernel(x_ref, cw_ref, cb_ref, wa_ref, ba_ref, wx_ref, bx_ref, lam_ref, o_ref,
                  xc_scr, h_scr, *, T, CL, TC, unroll, fast_sqrt):
    n_chunks, n_ctx = T // TC, CL // TC
    G = TC // SUBLANES
    cb = x_ref.shape[1]
    cw = cw_ref[...]
    taps = [(k - 2, cw[k:k + 1]) for k in range(RNN_CONV)]
    sub = lax.broadcasted_iota(jnp.int32, (G, SUBLANES, cb), 1)

    def conv_chunk(ci, _):
        t0 = pl.multiple_of(ci * TC, TC)
        seg_first = jnp.logical_or(ci == 0, ci == n_ctx)
        seg_last = jnp.logical_or(ci == n_ctx - 1, ci == n_chunks - 1)
        p0 = pl.multiple_of(jnp.maximum(t0 - HALO, 0), HALO)
        n0 = pl.multiple_of(jnp.minimum(t0 + TC, T - HALO), HALO)
        prev = x_ref[pl.ds(p0, HALO), :].astype(F32) * jnp.where(seg_first, 0.0, 1.0)
        nxt = x_ref[pl.ds(n0, HALO), :].astype(F32) * jnp.where(seg_last, 0.0, 1.0)
        main = x_ref[pl.ds(t0, TC), :].astype(F32)
        xc_scr[pl.ds(t0, TC), :] = _conv_rows(jnp.concatenate([prev, main, nxt], axis=0), taps, TC) + cb_ref[...]
        return 0

    lax.fori_loop(0, n_chunks, conv_chunk, 0)

    def sweep(fwd):
        d = 0 if fwd else 1
        neg_lam = -lam_ref[d]
        softplus = jnp.maximum(neg_lam, 0.0) + jnp.log(1.0 + jnp.exp(-jnp.abs(neg_lam)))

        def chunk(step, carry):
            if fwd:
                ci = step
            else:
                ci = jnp.where(step < n_ctx, n_ctx - 1 - step, n_chunks - 1 - (step - n_ctx))
            t0 = pl.multiple_of(ci * TC, TC)
            xc = xc_scr[pl.ds(t0, TC), :]
            xb = xc.astype(BF16)
            r = _sigmoid(jnp.dot(xb, wa_ref[d], preferred_element_type=F32) + ba_ref[d])
            gi = _sigmoid(jnp.dot(xb, wx_ref[d], preferred_element_type=F32) + bx_ref[d])
            log_a = (-RNN_C) * r * softplus
            a = jnp.exp(log_a)
            gap = 1.0 - a * a
            if fast_sqrt:
                root = jnp.where(gap > 0.0, gap * lax.rsqrt(gap), 0.0)
            else:
                root = jnp.sqrt(gap)
            bb = root * (gi * xc)
            a3 = a.reshape(G, SUBLANES, cb)
            b3 = bb.reshape(G, SUBLANES, cb)
            for s in (1, 2, 4):
                shift = s if fwd else SUBLANES - s
                use = (sub >= s) if fwd else (sub < SUBLANES - s)
                a_n, b_n = pltpu.roll(a3, shift, 1), pltpu.roll(b3, shift, 1)
                b3 = jnp.where(use, a3 * b_n + b3, b3)
                a3 = jnp.where(use, a3 * a_n, a3)

            hs = [None] * G
            for k in range(G):
                g = k if fwd else G - 1 - k
                hs[g] = b3[g] + a3[g] * carry
                carry = hs[g][SUBLANES - 1:SUBLANES] if fwd else hs[g][0:1]
            h = jnp.concatenate(hs, axis=0)
            if fwd:
                h_scr[pl.ds(t0, TC), :] = h
            else:
                o_ref[pl.ds(t0, TC), :] = (h_scr[pl.ds(t0, TC), :] + h).astype(o_ref.dtype)
            return carry

        lax.fori_loop(0, n_chunks, chunk, jnp.zeros((1, cb), F32), unroll=unroll)

    sweep(True)
    sweep(False)


def _rglru(p3, conv_w, conv_b, wa, ba, wx, bx, lam, cl, unroll, fast_sqrt):
    B, T, _ = p3.shape
    D = conv_w.shape[1]
    nb = D // MXU_DIM
    vec = lambda: pl.BlockSpec((2, 1, MXU_DIM), lambda b, c: (0, 0, c))
    mat = lambda: pl.BlockSpec((2, None, MXU_DIM, MXU_DIM), lambda b, c: (0, c, 0, 0))
    return pl.pallas_call(
        functools.partial(_rglru_kernel, T=T, CL=cl, TC=RNN_CHUNK, unroll=unroll, fast_sqrt=fast_sqrt),
        grid=(B, nb),
        in_specs=[pl.BlockSpec((None, T, MXU_DIM), lambda b, c: (b, 0, COL_RX * nb + c)),
                  pl.BlockSpec((RNN_CONV, MXU_DIM), lambda b, c: (0, c)),
                  pl.BlockSpec((1, MXU_DIM), lambda b, c: (0, c)),
                  mat(), vec(), mat(), vec(), vec()],
        out_specs=pl.BlockSpec((None, T, MXU_DIM), lambda b, c: (b, 0, c)),
        out_shape=jax.ShapeDtypeStruct((B, T, D), BF16),
        scratch_shapes=[pltpu.VMEM((T, MXU_DIM), F32), pltpu.VMEM((T, MXU_DIM), F32)],
        compiler_params=_params("parallel", "parallel"),
        name="rglru",
    )(p3, conv_w, conv_b.reshape(1, D), wa, ba.reshape(2, 1, D), wx, bx.reshape(2, 1, D),
      lam.reshape(2, 1, D))


def _attn_kernel(q_ref, k_ref, v_ref, lq_ref, g_ref, o_ref, vt_scr, sa0, sa1, sb0, sb1, *, T, CL, lam_init, do_ctx, kc, order):
    TQ = ATTN_TQ
    n_lat = (T - CL) // TQ
    for c in range(T // TQ):
        vt_scr[:, c * TQ:(c + 1) * TQ] = v_ref[c * TQ:(c + 1) * TQ, :].T

    lq = lq_ref[...]
    lam = (jnp.exp(jnp.sum(lq[0:1] * lq[1:2], axis=1, keepdims=True))
           - jnp.exp(jnp.sum(lq[2:3] * lq[3:4], axis=1, keepdims=True)) + lam_init)

    def step(key_rows, fin, nxt):
        if nxt is not None:
            r_n, s_n = nxt
            qt = (q_ref[pl.ds(r_n, TQ), :].astype(F32) * (HEAD_DIM ** -0.5 * math.log2(math.e))).T
            row = lax.broadcasted_iota(jnp.int32, qt.shape, 0)
            map0 = (row % HEAD_DIM) < (HEAD_DIM // 2)
            qts = (jnp.where(map0, qt, 0.0).astype(BF16), jnp.where(map0, 0.0, qt).astype(BF16))
            m_n = [jnp.full((1, TQ), -1e30, F32)] * 2
        if fin is not None:
            r_f, s_f, m_f = fin
            l_f = [jnp.zeros((1, TQ), F32)] * 2
            acc = [jnp.zeros((VALUE_DIM, TQ), F32)] * 2
        for k0 in range(0, key_rows, kc):
            size = min(kc, key_rows - k0)
            def finish_chunk():
                vt = vt_scr[:, k0:k0 + size]
                ps = []
                for i in range(2):
                    p = jnp.exp2(s_f[i][k0:k0 + size, :] - m_f[i])
                    l_f[i] = l_f[i] + jnp.sum(p, axis=0, keepdims=True)
                    if order == 1:
                        ps.append(p.astype(BF16))
                    else:
                        acc[i] = acc[i] + jnp.dot(vt, p.astype(BF16), preferred_element_type=F32)
                for i, pb in enumerate(ps):
                    acc[i] = acc[i] + jnp.dot(vt, pb, preferred_element_type=F32)

            def score_chunk():
                kt = k_ref[k0:k0 + size, :]
                for i in range(2):
                    s = jnp.dot(kt, qts[i], preferred_element_type=F32)
                    s_n[i][k0:k0 + size, :] = s
                    m_n[i] = jnp.maximum(m_n[i], jnp.max(s, axis=0, keepdims=True))

            parts = ([score_chunk] if nxt is not None else []) + ([finish_chunk] if fin is not None else [])
            for part in (parts if order == 2 else parts[::-1]):
                part()
        if fin is not None:
            o = acc[0] / l_f[0] - lam * (acc[1] / l_f[1])
            y = o * lax.rsqrt(jnp.mean(o * o, axis=0, keepdims=True) + NORM_EPS) * g_ref[...]
            o_ref[pl.ds(r_f, TQ), :] = (y * (1.0 - lam_init)).T.astype(o_ref.dtype)
        return tuple(m_n) if nxt is not None else None

    buf_a, buf_b = (sa0, sa1), (sb0, sb1)
    if do_ctx:
        for t in range(CL // TQ):
            m = step(CL, None, (t * TQ, buf_a))
            step(CL, (t * TQ, buf_a, m), None)
    else:
        o_ref[0:CL, :] = jnp.zeros((CL, VALUE_DIM), o_ref.dtype)

    def pair(jj, ma):
        r = pl.multiple_of(CL + 2 * jj * TQ, TQ)
        mb = step(T, (r, buf_a, ma), (r + TQ, buf_b))
        return step(T, (r + TQ, buf_b, mb), (r + 2 * TQ, buf_a))

    ma = lax.fori_loop(0, n_lat // 2 - 1, pair, step(T, None, (CL, buf_a)))
    r = T - 2 * TQ
    mb = step(T, (r, buf_a, ma), (r + TQ, buf_b))
    step(T, (r + TQ, buf_b, mb), None)


def _attention(p3, lq, g_subln, cl, lam_init, do_ctx, kc, order):
    B, T, _ = p3.shape
    assert kc % LANES == 0 and cl % LANES == 0 and T % LANES == 0
    D = ATTN_HEADS * VALUE_DIM
    cpb = D // VALUE_DIM
    seq = lambda col: pl.BlockSpec((None, T, VALUE_DIM), lambda b, h: (b, 0, col * cpb + h))
    return pl.pallas_call(
        functools.partial(_attn_kernel, T=T, CL=cl, lam_init=lam_init, do_ctx=do_ctx, kc=kc, order=order),
        grid=(B, ATTN_HEADS),
        in_specs=[seq(COL_Q), seq(COL_K), seq(COL_V),
                  pl.BlockSpec((4, HEAD_DIM), lambda b, h: (0, 0)),
                  pl.BlockSpec((VALUE_DIM, 1), lambda b, h: (0, 0))],
        out_specs=pl.BlockSpec((None, T, VALUE_DIM), lambda b, h: (b, 0, h)),
        out_shape=jax.ShapeDtypeStruct((B, T, D), BF16),
        scratch_shapes=[pltpu.VMEM((VALUE_DIM, T), BF16)] + [pltpu.VMEM((T, ATTN_TQ), F32)] * 4,
        compiler_params=_params("parallel", "parallel"),
        name="diff_attention",
    )(p3, p3, p3, lq, g_subln.reshape(VALUE_DIM, 1))


def _conv_kernel(v_ref, vp_ref, vn_ref, g_ref, gp_ref, gn_ref, w_ref, b_ref, lg_ref, lb_ref, o_ref,
                 *, T, CL, TC):
    i = pl.program_id(1)
    n_ctx, n_chunks = CL // TC, T // TC
    seg_first = jnp.logical_or(i == 0, i == n_ctx)
    seg_last = jnp.logical_or(i == n_ctx - 1, i == n_chunks - 1)

    def gated(v, g):
        return v[...].astype(F32) * _sigmoid(g[...].astype(F32))

    ext = jnp.concatenate([gated(vp_ref, gp_ref) * jnp.where(seg_first, 0.0, 1.0),
                           gated(v_ref, g_ref),
                           gated(vn_ref, gn_ref) * jnp.where(seg_last, 0.0, 1.0)], axis=0)
    w = w_ref[...]
    cols = []
    for c in range(ext.shape[1] // LANES):
        sl = slice(c * LANES, (c + 1) * LANES)
        taps = [(k - (CONV_K - 1) // 2, w[k:k + 1, sl]) for k in range(CONV_K)]
        cols.append(_conv_rows(ext[:, sl], taps, TC))
    z = jnp.concatenate(cols, axis=1) + b_ref[...]
    mu = jnp.mean(z, axis=-1, keepdims=True)
    zc = z - mu
    var = jnp.mean(zc * zc, axis=-1, keepdims=True)
    y = zc * lax.rsqrt(var + NORM_EPS) * lg_ref[...] + lb_ref[...]
    o_ref[...] = _silu(y).astype(o_ref.dtype)


def _conformer_conv(p3, dw_w, dw_b, ln_g, ln_b, cl):
    B, T, _ = p3.shape
    D = dw_w.shape[1]
    TC = ROW_TILE
    hb = TC // HALO
    last = T // HALO - 1

    def main(col):
        return pl.BlockSpec((None, TC, D), lambda b, i: (b, i, col))

    def prev(col):
        return pl.BlockSpec((None, HALO, D), lambda b, i: (b, jnp.maximum(i * hb - 1, 0), col))

    def nxt(col):
        return pl.BlockSpec((None, HALO, D), lambda b, i: (b, jnp.minimum((i + 1) * hb, last), col))

    vec = lambda: pl.BlockSpec((1, D), lambda b, i: (0, 0))
    return pl.pallas_call(
        functools.partial(_conv_kernel, T=T, CL=cl, TC=TC),
        grid=(B, T // TC),
        in_specs=[main(COL_CV), prev(COL_CV), nxt(COL_CV), main(COL_CG), prev(COL_CG), nxt(COL_CG),
                  pl.BlockSpec((CONV_K, D), lambda b, i: (0, 0)), vec(), vec(), vec()],
        out_specs=pl.BlockSpec((None, TC, D), lambda b, i: (b, i, 0)),
        out_shape=jax.ShapeDtypeStruct((B, T, D), BF16),
        compiler_params=_params("parallel", "parallel"),
        name="conformer_conv",
    )(p3, p3, p3, p3, p3, p3, dw_w, dw_b.reshape(1, D), ln_g.reshape(1, D), ln_b.reshape(1, D))


def _token_kernel(hr_ref, rg_ref, oa_ref, zc_ref, g0_ref, g1_ref, g2_ref, h_ref, m_ref, wr_ref, wa_ref, wc_ref,
                  wo_ref, gn_ref, wi_ref, wf_ref, *rest, d_ff, final):
    f32 = lambda r: r[...].astype(F32)
    m = m_ref[...]
    rec = f32(hr_ref) * _gelu_tanh(f32(rg_ref))
    y_r = jnp.dot(rec.astype(BF16), wr_ref[...], preferred_element_type=F32)
    y_a = jnp.dot(oa_ref[...], wa_ref[...], preferred_element_type=F32)
    y_c = jnp.dot(zc_ref[...], wc_ref[...], preferred_element_type=F32)
    mix = _sigmoid(f32(g0_ref)) * y_r + _sigmoid(f32(g1_ref)) * y_a + _sigmoid(f32(g2_ref)) * y_c
    h = h_ref[...] + m[2:3] * jnp.dot(mix.astype(BF16), wo_ref[...], preferred_element_type=F32)
    u = _mod_norm(h, gn_ref[...], m[3:4], m[4:5]).astype(BF16)
    gu = jnp.dot(u, wi_ref[...], preferred_element_type=F32)
    act = (_silu(gu[:, :d_ff]) * gu[:, d_ff:]).astype(BF16)
    h = h + m[5:6] * jnp.dot(act, wf_ref[...], preferred_element_type=F32)
    if final:
        gf_ref, o_ref = rest
        o_ref[...] = h * lax.rsqrt(jnp.mean(h * h, axis=-1, keepdims=True) + NORM_EPS) * gf_ref[...]
    else:
        (o_ref,) = rest
        o_ref[...] = h


def _token_update(hr, p3, oa, zc, h, modtab, w_r, w_a, w_c, w_o, g2, w_i, w_f, cl, first_tile, g_final=None):
    B, T, D = h.shape
    d_ff = w_f.shape[0]
    seg_of = _seg_index(cl // ROW_TILE)
    seg = lambda i: seg_of(i + first_tile)
    tile = lambda col: pl.BlockSpec((None, ROW_TILE, D), lambda b, i: (b, i + first_tile, col))
    once = lambda shape: pl.BlockSpec(shape, lambda b, i: (0, 0), pipeline_mode=pl.Buffered(1))
    in_specs = [tile(0), tile(COL_RG), tile(0), tile(0), tile(COL_G), tile(COL_G + 1), tile(COL_G + 2), tile(0),
                pl.BlockSpec((None, None, 8, D), lambda b, i: (b, seg(i), 0, 0)),
                once((D, D)), once((D, D)), once((D, D)), once((D, D)),
                once((1, D)), once((D, 2 * d_ff)), once((d_ff, D))]
    args = [hr, p3, oa, zc, p3, p3, p3, h, modtab, w_r, w_a, w_c, w_o, g2.reshape(1, D), w_i, w_f]
    grid = (B, T // ROW_TILE - first_tile)
    if g_final is None:
        return pl.pallas_call(
            functools.partial(_token_kernel, d_ff=d_ff, final=False), grid=grid, in_specs=in_specs,
            out_specs=tile(0), out_shape=jax.ShapeDtypeStruct((B, T, D), F32), input_output_aliases={7: 0},
            compiler_params=_params("parallel", "parallel"), name="token_update",
        )(*args)
    assert first_tile * ROW_TILE == cl
    return pl.pallas_call(
        functools.partial(_token_kernel, d_ff=d_ff, final=True), grid=grid, in_specs=in_specs + [once((1, D))],
        out_specs=pl.BlockSpec((None, ROW_TILE, D), lambda b, i: (b, i, 0)),
        out_shape=jax.ShapeDtypeStruct((B, T - cl, D), F32),
        compiler_params=_params("parallel", "parallel"), name="token_update_final",
    )(*args, g_final.reshape(1, D))


def _rope_tables(cl, s):
    pairs_axis = HEAD_DIM // 4
    rows = jnp.repeat(jnp.arange(s // GRID_W, dtype=F32), GRID_W)
    cols = jnp.tile(jnp.arange(GRID_W, dtype=F32), s // GRID_W)
    inv = ROPE_BASE ** (-jnp.arange(pairs_axis, dtype=F32) / pairs_axis)
    ang = jnp.concatenate([rows[:, None] * inv, cols[:, None] * inv], axis=-1)
    cos = jnp.concatenate([jnp.ones((cl, HEAD_DIM // 2), F32), jnp.cos(ang)], axis=0)
    sin = jnp.concatenate([jnp.zeros((cl, HEAD_DIM // 2), F32), jnp.sin(ang)], axis=0)
    return (jnp.concatenate([cos, cos, cos, cos], axis=1),
            jnp.concatenate([-sin, -sin, sin, sin], axis=1))


def _rope_layout(a):
    lead = a.shape[:-1]
    a = a.reshape(lead + (ATTN_HEADS, 2, 2, HEAD_DIM // 2))
    return jnp.swapaxes(a, -3, -2).reshape(lead + (ATTN_HEADS * VALUE_DIM,))


def _permute_qk_columns(a):
    d = a.shape[-1] // N_COL_BLOCKS
    blocks = [a[..., i * d:(i + 1) * d] for i in range(N_COL_BLOCKS)]
    for i in (COL_K, COL_Q):
        blocks[i] = _rope_layout(blocks[i])
    return jnp.concatenate(blocks, axis=-1)


def _blockdiag_tiles(w):
    two, nb, bs, _ = w.shape
    per = MXU_DIM // bs
    w = w.reshape(two, nb // per, per, bs, bs)
    eye = jnp.eye(per, dtype=w.dtype)
    t = jnp.einsum('dtpio,pq->dtpiqo', w, eye)
    return t.reshape(two, nb // per, MXU_DIM, MXU_DIM)


def kernel(x, c, ctx, c_ctx, w_mod, b_mod, g_norm1, g_norm2, w_in, b_in, rnn_conv_w, rnn_conv_b, rnn_w_a, rnn_b_a, rnn_w_x, rnn_b_x, rnn_lambda, w_rnn_o, lambda_qk, g_subln, w_attn_o, conv_dw_w, conv_dw_b, conv_ln_g, conv_ln_b, w_conv_o, w_out, w_ffn_in, w_ffn_out, g_final):
    B, S, D = x.shape
    CL = ctx.shape[1]
    T = CL + S
    L = w_mod.shape[0]
    assert D == ATTN_HEADS * VALUE_DIM and S % GRID_W == 0
    assert CL % ROW_TILE == 0 and S % ROW_TILE == 0 and ROW_TILE == ATTN_TQ
    assert (S // ATTN_TQ) % 2 == 0

    h = jnp.concatenate([ctx, x], axis=1)

    n_cond = -(-(B + 1) // SUBLANES) * SUBLANES
    cc = jnp.zeros((n_cond, D), F32).at[:B].set(c).at[B].set(c_ctx)
    mod = _modulation(cc, w_mod, b_mod).reshape(L, n_cond, N_MOD, D)
    mod_ctx = jnp.broadcast_to(mod[:, B][:, None], (L, B, N_MOD, D))
    modtab = jnp.stack([mod_ctx, mod[:, :B]], axis=2)
    modtab = jnp.pad(modtab, ((0, 0), (0, 0), (0, 0), (0, 8 - N_MOD), (0, 0)))

    cos_t, sin_t = _rope_tables(CL, S)
    w_in_b = _permute_qk_columns(w_in).astype(BF16)
    b_in_p = _permute_qk_columns(b_in)

    for l in range(L):
        lam_init = 0.8 - 0.6 * math.exp(-0.3 * l)
        first_tile = CL // ROW_TILE if l == L - 1 else 0
        p = _in_proj(h.reshape(B * T, D), g_norm1[l], modtab[l], w_in_b[l], b_in_p[l], cos_t, sin_t, CL,
                     T // PROJ_TILES, PROJ_SPLIT_PLAN[min(l, 3)])
        p3 = p.reshape(B, T, N_COL_BLOCKS * D)
        hr = _rglru(p3, rnn_conv_w[l], rnn_conv_b[l],
                    _blockdiag_tiles(rnn_w_a[l]).astype(BF16), rnn_b_a[l],
                    _blockdiag_tiles(rnn_w_x[l]).astype(BF16), rnn_b_x[l], rnn_lambda[l], CL, *RNN_PLAN[min(l, 3)])
        oa = _attention(p3, lambda_qk[l], g_subln[l], CL, lam_init, first_tile == 0, ATTN_KC,
                        ATTN_ORDER_PLAN[min(l, 3)])
        zc = _conformer_conv(p3, conv_dw_w[l], conv_dw_b[l], conv_ln_g[l], conv_ln_b[l], CL)
        h = _token_update(hr, p3, oa, zc, h, modtab[l], w_rnn_o[l].astype(BF16), w_attn_o[l].astype(BF16),
                          w_conv_o[l].astype(BF16), w_out[l].astype(BF16), g_norm2[l], w_ffn_in[l].astype(BF16),
                          w_ffn_out[l].astype(BF16), CL, first_tile, g_final if l == L - 1 else None)
    return h
```

```python
import functools
import math

import jax
import jax.numpy as jnp
from jax import lax
from jax.experimental import pallas as pl
from jax.experimental.pallas import tpu as pltpu

F32 = jnp.float32
BF16 = jnp.bfloat16

NORM_EPS = 1e-6
N_MOD = 6
ATTN_HEADS = 8
HEAD_DIM = 64
VALUE_DIM = 2 * HEAD_DIM
GRID_W = 64
ROPE_BASE = 10000.0
RNN_BLOCK = 64
RNN_CONV = 4
RNN_C = 8.0
CONV_K = 31
LANES = 128
SUBLANES = 8
BF16_ROWS = 16
HALO = 16
MXU_DIM = 256
VMEM_LIMIT = 56 * 1024 * 1024

COL_RX, COL_K, COL_V, COL_RG, COL_Q, COL_CV, COL_CG, COL_G = 0, 1, 2, 3, 4, 5, 6, 7
N_COL_BLOCKS = 10

ROW_TILE = 256
RNN_CHUNK = 256
ATTN_TQ = 256
ATTN_KC = 256
PROJ_TILES = 2
PROJ_SPLIT = 4


def _sigmoid(v):
    return 1.0 / (1.0 + jnp.exp(-v))


def _silu(v):
    return v * _sigmoid(v)


def _gelu_tanh(v):
    return 0.5 * v * (1.0 + jnp.tanh(math.sqrt(2.0 / math.pi) * (v + 0.044715 * (v * v * v))))


def _params(*sem):
    return pltpu.CompilerParams(dimension_semantics=sem, vmem_limit_bytes=VMEM_LIMIT)


def _mod_kernel(c_ref, w_ref, b_ref, o_ref):
    s = _silu(c_ref[...])
    o_ref[...] = jnp.dot(s, w_ref[...], preferred_element_type=F32,
                         precision=lax.Precision.HIGHEST) + b_ref[...]


def _modulation(cc, w_mod, b_mod):
    L, D, _ = w_mod.shape
    R = cc.shape[0]
    return pl.pallas_call(
        _mod_kernel,
        grid=(L, N_MOD),
        in_specs=[pl.BlockSpec((R, D), lambda l, j: (0, 0)),
                  pl.BlockSpec((None, D, D), lambda l, j: (l, 0, j)),
                  pl.BlockSpec((None, 1, D), lambda l, j: (l, 0, j))],
        out_specs=pl.BlockSpec((None, R, D), lambda l, j: (l, 0, j)),
        out_shape=jax.ShapeDtypeStruct((L, R, N_MOD * D), F32),
        compiler_params=_params("arbitrary", "arbitrary"),
        name="modulation",
    )(cc, w_mod, b_mod.reshape(L, 1, N_MOD * D))


def _seg_index(n_ctx_tiles):
    return lambda i: jnp.where(i >= n_ctx_tiles, 1, 0)


def _mod_norm(x, g, shift, scale):
    y = x * lax.rsqrt(jnp.mean(x * x, axis=-1, keepdims=True) + NORM_EPS) * g
    return y * (1.0 + scale) + shift


def _rope(t, cos, sin_signed):
    outs = []
    for k in range(t.shape[1] // LANES):
        tk = t[:, k * LANES:(k + 1) * LANES]
        outs.append(tk * cos + pltpu.roll(tk, LANES // 2, 1) * sin_signed)
    return jnp.concatenate(outs, axis=1)


def _proj_kernel(h_ref, g_ref, m_ref, w_ref, b_ref, cos_ref, sin_ref, o_ref, u_scr, *, CL, tiles_per_seq, split):
    i, j = pl.program_id(0), pl.program_id(1)
    tm = h_ref.shape[0]

    @pl.when(j == 0)
    def _():
        m = m_ref[...]
        row = (i % tiles_per_seq) * tm + lax.broadcasted_iota(jnp.int32, (tm, 1), 0)
        is_ctx = row < CL
        shift = jnp.where(is_ctx, m[0, 0:1], m[1, 0:1])
        scale = jnp.where(is_ctx, m[0, 1:2], m[1, 1:2])
        u_scr[...] = _mod_norm(h_ref[...], g_ref[...], shift, scale).astype(u_scr.dtype)

    is_rope = jnp.logical_or(j == COL_K, j == COL_Q)
    rb = tm // split

    def project(k):
        sl = slice(k * rb, (k + 1) * rb)
        return sl, jnp.dot(u_scr[sl, :], w_ref[...], preferred_element_type=F32) + b_ref[...]

    @pl.when(is_rope)
    def _():
        for k in range(split):
            sl, acc = project(k)
            o_ref[sl, :] = _rope(acc, cos_ref[sl, :], sin_ref[sl, :]).astype(o_ref.dtype)

    @pl.when(jnp.logical_not(is_rope))
    def _():
        for k in range(split):
            sl, acc = project(k)
            o_ref[sl, :] = acc.astype(o_ref.dtype)


def _in_proj(h2, g, modtab, w, b, cos_t, sin_t, cl, tm, split):
    M, D = h2.shape
    N = w.shape[1]
    T = cos_t.shape[0]
    tiles_per_seq = T // tm
    assert T % tm == 0 and tm % (split * BF16_ROWS) == 0
    return pl.pallas_call(
        functools.partial(_proj_kernel, CL=cl, tiles_per_seq=tiles_per_seq, split=split),
        grid=(M // tm, N // D),
        in_specs=[pl.BlockSpec((tm, D), lambda i, j: (i, 0)),
                  pl.BlockSpec((1, D), lambda i, j: (0, 0)),
                  pl.BlockSpec((None, 2, 8, D), lambda i, j: (i // tiles_per_seq, 0, 0, 0)),
                  pl.BlockSpec((D, D), lambda i, j: (0, j)),
                  pl.BlockSpec((1, D), lambda i, j: (0, j)),
                  pl.BlockSpec((tm, LANES), lambda i, j: (i % tiles_per_seq, 0)),
                  pl.BlockSpec((tm, LANES), lambda i, j: (i % tiles_per_seq, 0))],
        out_specs=pl.BlockSpec((tm, D), lambda i, j: (i, j)),
        out_shape=jax.ShapeDtypeStruct((M, N), BF16),
        scratch_shapes=[pltpu.VMEM((tm, D), BF16)],
        compiler_params=_params("parallel", "arbitrary"),
        name="in_proj",
    )(h2, g.reshape(1, D), modtab, w, b.reshape(1, N), cos_t, sin_t)


def _conv_rows(ext, taps, rows_out):
    n = ext.shape[0]
    rolled = {}
    acc = None
    for off, w in taps:
        start = HALO + off
        r, q = start % SUBLANES, start // SUBLANES
        if r not in rolled:
            rolled[r] = ext if r == 0 else pltpu.roll(ext, n - r, 0)
        term = w * rolled[r][q * SUBLANES:q * SUBLANES + rows_out]
        acc = term if acc is None else acc + term
    return acc


def _rglru_kernel(x_ref, cw_ref, cb_ref, wa_ref, ba_ref, wx_ref, bx_ref, lam_ref, o_ref,
                  xc_scr, h_scr, *, T, CL, TC):
    n_chunks, n_ctx = T // TC, CL // TC
    G = TC // SUBLANES
    cb = x_ref.shape[1]
    cw = cw_ref[...]
    taps = [(k - 2, cw[k:k + 1]) for k in range(RNN_CONV)]
    sub = lax.broadcasted_iota(jnp.int32, (G, SUBLANES, cb), 1)

    def conv_chunk(ci, _):
        t0 = pl.multiple_of(ci * TC, TC)
        seg_first = jnp.logical_or(ci == 0, ci == n_ctx)
        seg_last = jnp.logical_or(ci == n_ctx - 1, ci == n_chunks - 1)
        p0 = pl.multiple_of(jnp.maximum(t0 - HALO, 0), HALO)
        n0 = pl.multiple_of(jnp.minimum(t0 + TC, T - HALO), HALO)
        prev = x_ref[pl.ds(p0, HALO), :].astype(F32) * jnp.where(seg_first, 0.0, 1.0)
        nxt = x_ref[pl.ds(n0, HALO), :].astype(F32) * jnp.where(seg_last, 0.0, 1.0)
        main = x_ref[pl.ds(t0, TC), :].astype(F32)
        xc_scr[pl.ds(t0, TC), :] = _conv_rows(jnp.concatenate([prev, main, nxt], axis=0), taps, TC) + cb_ref[...]
        return 0

    lax.fori_loop(0, n_chunks, conv_chunk, 0)

    def sweep(fwd):
        d = 0 if fwd else 1
        neg_lam = -lam_ref[d]
        softplus = jnp.maximum(neg_lam, 0.0) + jnp.log(1.0 + jnp.exp(-jnp.abs(neg_lam)))

        def chunk(step, carry):
            if fwd:
                ci = step
            else:
                ci = jnp.where(step < n_ctx, n_ctx - 1 - step, n_chunks - 1 - (step - n_ctx))
            t0 = pl.multiple_of(ci * TC, TC)
            xc = xc_scr[pl.ds(t0, TC), :]
            xb = xc.astype(BF16)
            r = _sigmoid(jnp.dot(xb, wa_ref[d], preferred_element_type=F32) + ba_ref[d])
            gi = _sigmoid(jnp.dot(xb, wx_ref[d], preferred_element_type=F32) + bx_ref[d])
            log_a = (-RNN_C) * r * softplus
            a = jnp.exp(log_a)
            gap = 1.0 - a * a
            root = jnp.where(gap > 0.0, gap * lax.rsqrt(gap), 0.0)
            bb = root * (gi * xc)
            a3 = a.reshape(G, SUBLANES, cb)
            b3 = bb.reshape(G, SUBLANES, cb)
            for s in (1, 2, 4):
                shift = s if fwd else SUBLANES - s
                use = (sub >= s) if fwd else (sub < SUBLANES - s)
                a_n, b_n = pltpu.roll(a3, shift, 1), pltpu.roll(b3, shift, 1)
                b3 = jnp.where(use, a3 * b_n + b3, b3)
                a3 = jnp.where(use, a3 * a_n, a3)

            hs = [None] * G
            for k in range(G):
                g = k if fwd else G - 1 - k
                hs[g] = b3[g] + a3[g] * carry
                carry = hs[g][SUBLANES - 1:SUBLANES] if fwd else hs[g][0:1]
            h = jnp.concatenate(hs, axis=0)
            if fwd:
                h_scr[pl.ds(t0, TC), :] = h
            else:
                o_ref[pl.ds(t0, TC), :] = (h_scr[pl.ds(t0, TC), :] + h).astype(o_ref.dtype)
            return carry

        lax.fori_loop(0, n_chunks, chunk, jnp.zeros((1, cb), F32), unroll=True)

    sweep(True)
    sweep(False)


def _rglru(p3, conv_w, conv_b, wa, ba, wx, bx, lam, cl):
    B, T, _ = p3.shape
    D = conv_w.shape[1]
    nb = D // MXU_DIM
    vec = lambda: pl.BlockSpec((2, 1, MXU_DIM), lambda b, c: (0, 0, c))
    mat = lambda: pl.BlockSpec((2, None, MXU_DIM, MXU_DIM), lambda b, c: (0, c, 0, 0))
    return pl.pallas_call(
        functools.partial(_rglru_kernel, T=T, CL=cl, TC=RNN_CHUNK),
        grid=(B, nb),
        in_specs=[pl.BlockSpec((None, T, MXU_DIM), lambda b, c: (b, 0, COL_RX * nb + c)),
                  pl.BlockSpec((RNN_CONV, MXU_DIM), lambda b, c: (0, c)),
                  pl.BlockSpec((1, MXU_DIM), lambda b, c: (0, c)),
                  mat(), vec(), mat(), vec(), vec()],
        out_specs=pl.BlockSpec((None, T, MXU_DIM), lambda b, c: (b, 0, c)),
        out_shape=jax.ShapeDtypeStruct((B, T, D), BF16),
        scratch_shapes=[pltpu.VMEM((T, MXU_DIM), F32), pltpu.VMEM((T, MXU_DIM), F32)],
        compiler_params=_params("parallel", "parallel"),
        name="rglru",
    )(p3, conv_w, conv_b.reshape(1, D), wa, ba.reshape(2, 1, D), wx, bx.reshape(2, 1, D),
      lam.reshape(2, 1, D))


def _attn_kernel(q_ref, k_ref, v_ref, lq_ref, g_ref, o_ref, vt_scr, sa0, sa1, sb0, sb1, *, T, CL, lam_init, do_ctx, kc):
    TQ = ATTN_TQ
    n_lat = (T - CL) // TQ
    for c in range(T // TQ):
        vt_scr[:, c * TQ:(c + 1) * TQ] = v_ref[c * TQ:(c + 1) * TQ, :].T

    lq = lq_ref[...]
    lam = (jnp.exp(jnp.sum(lq[0:1] * lq[1:2], axis=1, keepdims=True))
           - jnp.exp(jnp.sum(lq[2:3] * lq[3:4], axis=1, keepdims=True)) + lam_init)

    def step(key_rows, fin, nxt):
        if nxt is not None:
            r_n, s_n = nxt
            qt = (q_ref[pl.ds(r_n, TQ), :].astype(F32) * (HEAD_DIM ** -0.5 * math.log2(math.e))).T
            row = lax.broadcasted_iota(jnp.int32, qt.shape, 0)
            map0 = (row % HEAD_DIM) < (HEAD_DIM // 2)
            qts = (jnp.where(map0, qt, 0.0).astype(BF16), jnp.where(map0, 0.0, qt).astype(BF16))
            m_n = [jnp.full((1, TQ), -1e30, F32)] * 2
        if fin is not None:
            r_f, s_f, m_f = fin
            l_f = [jnp.zeros((1, TQ), F32)] * 2
            acc = [jnp.zeros((VALUE_DIM, TQ), F32)] * 2
        for k0 in range(0, key_rows, kc):
            size = min(kc, key_rows - k0)
            if fin is not None:
                vt = vt_scr[:, k0:k0 + size]
                for i in range(2):
                    p = jnp.exp2(s_f[i][k0:k0 + size, :] - m_f[i])
                    l_f[i] = l_f[i] + jnp.sum(p, axis=0, keepdims=True)
                    acc[i] = acc[i] + jnp.dot(vt, p.astype(BF16), preferred_element_type=F32)
            if nxt is not None:
                kt = k_ref[k0:k0 + size, :]
                for i in range(2):
                    s = jnp.dot(kt, qts[i], preferred_element_type=F32)
                    s_n[i][k0:k0 + size, :] = s
                    m_n[i] = jnp.maximum(m_n[i], jnp.max(s, axis=0, keepdims=True))
        if fin is not None:
            o = acc[0] / l_f[0] - lam * (acc[1] / l_f[1])
            y = o * lax.rsqrt(jnp.mean(o * o, axis=0, keepdims=True) + NORM_EPS) * g_ref[...]
            o_ref[pl.ds(r_f, TQ), :] = (y * (1.0 - lam_init)).T.astype(o_ref.dtype)
        return tuple(m_n) if nxt is not None else None

    buf_a, buf_b = (sa0, sa1), (sb0, sb1)
    if do_ctx:
        for t in range(CL // TQ):
            m = step(CL, None, (t * TQ, buf_a))
            step(CL, (t * TQ, buf_a, m), None)
    else:
        o_ref[0:CL, :] = jnp.zeros((CL, VALUE_DIM), o_ref.dtype)

    def pair(jj, ma):
        r = pl.multiple_of(CL + 2 * jj * TQ, TQ)
        mb = step(T, (r, buf_a, ma), (r + TQ, buf_b))
        return step(T, (r + TQ, buf_b, mb), (r + 2 * TQ, buf_a))

    ma = lax.fori_loop(0, n_lat // 2 - 1, pair, step(T, None, (CL, buf_a)))
    r = T - 2 * TQ
    mb = step(T, (r, buf_a, ma), (r + TQ, buf_b))
    step(T, (r + TQ, buf_b, mb), None)


def _attention(p3, lq, g_subln, cl, lam_init, do_ctx, kc):
    B, T, _ = p3.shape
    assert kc % LANES == 0 and cl % LANES == 0 and T % LANES == 0
    D = ATTN_HEADS * VALUE_DIM
    cpb = D // VALUE_DIM
    seq = lambda col: pl.BlockSpec((None, T, VALUE_DIM), lambda b, h: (b, 0, col * cpb + h))
    return pl.pallas_call(
        functools.partial(_attn_kernel, T=T, CL=cl, lam_init=lam_init, do_ctx=do_ctx, kc=kc),
        grid=(B, ATTN_HEADS),
        in_specs=[seq(COL_Q), seq(COL_K), seq(COL_V),
                  pl.BlockSpec((4, HEAD_DIM), lambda b, h: (0, 0)),
                  pl.BlockSpec((VALUE_DIM, 1), lambda b, h: (0, 0))],
        out_specs=pl.BlockSpec((None, T, VALUE_DIM), lambda b, h: (b, 0, h)),
        out_shape=jax.ShapeDtypeStruct((B, T, D), BF16),
        scratch_shapes=[pltpu.VMEM((VALUE_DIM, T), BF16)] + [pltpu.VMEM((T, ATTN_TQ), F32)] * 4,
        compiler_params=_params("parallel", "parallel"),
        name="diff_attention",
    )(p3, p3, p3, lq, g_subln.reshape(VALUE_DIM, 1))


def _conv_kernel(v_ref, vp_ref, vn_ref, g_ref, gp_ref, gn_ref, w_ref, b_ref, lg_ref, lb_ref, o_ref,
                 *, T, CL, TC):
    i = pl.program_id(1)
    n_ctx, n_chunks = CL // TC, T // TC
    seg_first = jnp.logical_or(i == 0, i == n_ctx)
    seg_last = jnp.logical_or(i == n_ctx - 1, i == n_chunks - 1)

    def gated(v, g):
        return v[...].astype(F32) * _sigmoid(g[...].astype(F32))

    ext = jnp.concatenate([gated(vp_ref, gp_ref) * jnp.where(seg_first, 0.0, 1.0),
                           gated(v_ref, g_ref),
                           gated(vn_ref, gn_ref) * jnp.where(seg_last, 0.0, 1.0)], axis=0)
    w = w_ref[...]
    cols = []
    for c in range(ext.shape[1] // LANES):
        sl = slice(c * LANES, (c + 1) * LANES)
        taps = [(k - (CONV_K - 1) // 2, w[k:k + 1, sl]) for k in range(CONV_K)]
        cols.append(_conv_rows(ext[:, sl], taps, TC))
    z = jnp.concatenate(cols, axis=1) + b_ref[...]
    mu = jnp.mean(z, axis=-1, keepdims=True)
    zc = z - mu
    var = jnp.mean(zc * zc, axis=-1, keepdims=True)
    y = zc * lax.rsqrt(var + NORM_EPS) * lg_ref[...] + lb_ref[...]
    o_ref[...] = _silu(y).astype(o_ref.dtype)


def _conformer_conv(p3, dw_w, dw_b, ln_g, ln_b, cl):
    B, T, _ = p3.shape
    D = dw_w.shape[1]
    TC = ROW_TILE
    hb = TC // HALO
    last = T // HALO - 1

    def main(col):
        return pl.BlockSpec((None, TC, D), lambda b, i: (b, i, col))

    def prev(col):
        return pl.BlockSpec((None, HALO, D), lambda b, i: (b, jnp.maximum(i * hb - 1, 0), col))

    def nxt(col):
        return pl.BlockSpec((None, HALO, D), lambda b, i: (b, jnp.minimum((i + 1) * hb, last), col))

    vec = lambda: pl.BlockSpec((1, D), lambda b, i: (0, 0))
    return pl.pallas_call(
        functools.partial(_conv_kernel, T=T, CL=cl, TC=TC),
        grid=(B, T // TC),
        in_specs=[main(COL_CV), prev(COL_CV), nxt(COL_CV), main(COL_CG), prev(COL_CG), nxt(COL_CG),
                  pl.BlockSpec((CONV_K, D), lambda b, i: (0, 0)), vec(), vec(), vec()],
        out_specs=pl.BlockSpec((None, TC, D), lambda b, i: (b, i, 0)),
        out_shape=jax.ShapeDtypeStruct((B, T, D), BF16),
        compiler_params=_params("parallel", "parallel"),
        name="conformer_conv",
    )(p3, p3, p3, p3, p3, p3, dw_w, dw_b.reshape(1, D), ln_g.reshape(1, D), ln_b.reshape(1, D))


def _token_kernel(hr_ref, rg_ref, oa_ref, zc_ref, g0_ref, g1_ref, g2_ref, h_ref, m_ref, wr_ref, wa_ref, wc_ref,
                  wo_ref, gn_ref, wi_ref, wf_ref, *rest, d_ff, final):
    f32 = lambda r: r[...].astype(F32)
    m = m_ref[...]
    rec = f32(hr_ref) * _gelu_tanh(f32(rg_ref))
    y_r = jnp.dot(rec.astype(BF16), wr_ref[...], preferred_element_type=F32)
    y_a = jnp.dot(oa_ref[...], wa_ref[...], preferred_element_type=F32)
    y_c = jnp.dot(zc_ref[...], wc_ref[...], preferred_element_type=F32)
    mix = _sigmoid(f32(g0_ref)) * y_r + _sigmoid(f32(g1_ref)) * y_a + _sigmoid(f32(g2_ref)) * y_c
    h = h_ref[...] + m[2:3] * jnp.dot(mix.astype(BF16), wo_ref[...], preferred_element_type=F32)
    u = _mod_norm(h, gn_ref[...], m[3:4], m[4:5]).astype(BF16)
    gu = jnp.dot(u, wi_ref[...], preferred_element_type=F32)
    act = (_silu(gu[:, :d_ff]) * gu[:, d_ff:]).astype(BF16)
    h = h + m[5:6] * jnp.dot(act, wf_ref[...], preferred_element_type=F32)
    if final:
        gf_ref, o_ref = rest
        o_ref[...] = h * lax.rsqrt(jnp.mean(h * h, axis=-1, keepdims=True) + NORM_EPS) * gf_ref[...]
    else:
        (o_ref,) = rest
        o_ref[...] = h


def _token_update(hr, p3, oa, zc, h, modtab, w_r, w_a, w_c, w_o, g2, w_i, w_f, cl, first_tile, g_final=None):
    B, T, D = h.shape
    d_ff = w_f.shape[0]
    seg_of = _seg_index(cl // ROW_TILE)
    seg = lambda i: seg_of(i + first_tile)
    tile = lambda col: pl.BlockSpec((None, ROW_TILE, D), lambda b, i: (b, i + first_tile, col))
    once = lambda shape: pl.BlockSpec(shape, lambda b, i: (0, 0), pipeline_mode=pl.Buffered(1))
    in_specs = [tile(0), tile(COL_RG), tile(0), tile(0), tile(COL_G), tile(COL_G + 1), tile(COL_G + 2), tile(0),
                pl.BlockSpec((None, None, 8, D), lambda b, i: (b, seg(i), 0, 0)),
                once((D, D)), once((D, D)), once((D, D)), once((D, D)),
                once((1, D)), once((D, 2 * d_ff)), once((d_ff, D))]
    args = [hr, p3, oa, zc, p3, p3, p3, h, modtab, w_r, w_a, w_c, w_o, g2.reshape(1, D), w_i, w_f]
    grid = (B, T // ROW_TILE - first_tile)
    if g_final is None:
        return pl.pallas_call(
            functools.partial(_token_kernel, d_ff=d_ff, final=False), grid=grid, in_specs=in_specs,
            out_specs=tile(0), out_shape=jax.ShapeDtypeStruct((B, T, D), F32), input_output_aliases={7: 0},
            compiler_params=_params("parallel", "parallel"), name="token_update",
        )(*args)
    assert first_tile * ROW_TILE == cl
    return pl.pallas_call(
        functools.partial(_token_kernel, d_ff=d_ff, final=True), grid=grid, in_specs=in_specs + [once((1, D))],
        out_specs=pl.BlockSpec((None, ROW_TILE, D), lambda b, i: (b, i, 0)),
        out_shape=jax.ShapeDtypeStruct((B, T - cl, D), F32),
        compiler_params=_params("parallel", "parallel"), name="token_update_final",
    )(*args, g_final.reshape(1, D))


def _rope_tables(cl, s):
    pairs_axis = HEAD_DIM // 4
    rows = jnp.repeat(jnp.arange(s // GRID_W, dtype=F32), GRID_W)
    cols = jnp.tile(jnp.arange(GRID_W, dtype=F32), s // GRID_W)
    inv = ROPE_BASE ** (-jnp.arange(pairs_axis, dtype=F32) / pairs_axis)
    ang = jnp.concatenate([rows[:, None] * inv, cols[:, None] * inv], axis=-1)
    cos = jnp.concatenate([jnp.ones((cl, HEAD_DIM // 2), F32), jnp.cos(ang)], axis=0)
    sin = jnp.concatenate([jnp.zeros((cl, HEAD_DIM // 2), F32), jnp.sin(ang)], axis=0)
    return (jnp.concatenate([cos, cos, cos, cos], axis=1),
            jnp.concatenate([-sin, -sin, sin, sin], axis=1))


def _rope_layout(a):
    lead = a.shape[:-1]
    a = a.reshape(lead + (ATTN_HEADS, 2, 2, HEAD_DIM // 2))
    return jnp.swapaxes(a, -3, -2).reshape(lead + (ATTN_HEADS * VALUE_DIM,))


def _permute_qk_columns(a):
    d = a.shape[-1] // N_COL_BLOCKS
    blocks = [a[..., i * d:(i + 1) * d] for i in range(N_COL_BLOCKS)]
    for i in (COL_K, COL_Q):
        blocks[i] = _rope_layout(blocks[i])
    return jnp.concatenate(blocks, axis=-1)


def _blockdiag_tiles(w):
    two, nb, bs, _ = w.shape
    per = MXU_DIM // bs
    w = w.reshape(two, nb // per, per, bs, bs)
    eye = jnp.eye(per, dtype=w.dtype)
    t = jnp.einsum('dtpio,pq->dtpiqo', w, eye)
    return t.reshape(two, nb // per, MXU_DIM, MXU_DIM)


def kernel(x, c, ctx, c_ctx, w_mod, b_mod, g_norm1, g_norm2, w_in, b_in, rnn_conv_w, rnn_conv_b, rnn_w_a, rnn_b_a, rnn_w_x, rnn_b_x, rnn_lambda, w_rnn_o, lambda_qk, g_subln, w_attn_o, conv_dw_w, conv_dw_b, conv_ln_g, conv_ln_b, w_conv_o, w_out, w_ffn_in, w_ffn_out, g_final):
    B, S, D = x.shape
    CL = ctx.shape[1]
    T = CL + S
    L = w_mod.shape[0]
    assert D == ATTN_HEADS * VALUE_DIM and S % GRID_W == 0
    assert CL % ROW_TILE == 0 and S % ROW_TILE == 0 and ROW_TILE == ATTN_TQ
    assert (S // ATTN_TQ) % 2 == 0

    h = jnp.concatenate([ctx, x], axis=1)

    n_cond = -(-(B + 1) // SUBLANES) * SUBLANES
    cc = jnp.zeros((n_cond, D), F32).at[:B].set(c).at[B].set(c_ctx)
    mod = _modulation(cc, w_mod, b_mod).reshape(L, n_cond, N_MOD, D)
    mod_ctx = jnp.broadcast_to(mod[:, B][:, None], (L, B, N_MOD, D))
    modtab = jnp.stack([mod_ctx, mod[:, :B]], axis=2)
    modtab = jnp.pad(modtab, ((0, 0), (0, 0), (0, 0), (0, 8 - N_MOD), (0, 0)))

    cos_t, sin_t = _rope_tables(CL, S)
    w_in_b = _permute_qk_columns(w_in).astype(BF16)
    b_in_p = _permute_qk_columns(b_in)

    for l in range(L):
        lam_init = 0.8 - 0.6 * math.exp(-0.3 * l)
        first_tile = CL // ROW_TILE if l == L - 1 else 0
        p = _in_proj(h.reshape(B * T, D), g_norm1[l], modtab[l], w_in_b[l], b_in_p[l], cos_t, sin_t, CL,
                     T // PROJ_TILES, PROJ_SPLIT)
        p3 = p.reshape(B, T, N_COL_BLOCKS * D)
        hr = _rglru(p3, rnn_conv_w[l], rnn_conv_b[l],
                    _blockdiag_tiles(rnn_w_a[l]).astype(BF16), rnn_b_a[l],
                    _blockdiag_tiles(rnn_w_x[l]).astype(BF16), rnn_b_x[l], rnn_lambda[l], CL)
        oa = _attention(p3, lambda_qk[l], g_subln[l], CL, lam_init, first_tile == 0, ATTN_KC)
        zc = _conformer_conv(p3, conv_dw_w[l], conv_dw_b[l], conv_ln_g[l], conv_ln_b[l], CL)
        h = _token_update(hr, p3, oa, zc, h, modtab[l], w_rnn_o[l].astype(BF16), w_attn_o[l].astype(BF16),
                          w_conv_o[l].astype(BF16), w_out[l].astype(BF16), g_norm2[l], w_ffn_in[l].astype(BF16),
                          w_ffn_out[l].astype(BF16), CL, first_tile, g_final if l == L - 1 else None)
    return h
```

```python
import functools
import math

import jax
import jax.numpy as jnp
from jax import lax
from jax.experimental import pallas as pl
from jax.experimental.pallas import tpu as pltpu

F32 = jnp.float32
BF16 = jnp.bfloat16

NORM_EPS = 1e-6
N_MOD = 6
ATTN_HEADS = 8
HEAD_DIM = 64
VALUE_DIM = 2 * HEAD_DIM
GRID_W = 64
ROPE_BASE = 10000.0
RNN_BLOCK = 64
RNN_CONV = 4
RNN_C = 8.0
CONV_K = 31
LANES = 128
SUBLANES = 8
BF16_ROWS = 16
HALO = 16
MXU_DIM = 256
VMEM_LIMIT = 56 * 1024 * 1024

COL_RX, COL_K, COL_V, COL_RG, COL_Q, COL_CV, COL_CG, COL_G = 0, 1, 2, 3, 4, 5, 6, 7
N_COL_BLOCKS = 10

ROW_TILE = 256
RNN_CHUNK = 256
ATTN_TQ = 256
ATTN_KC = 256
PROJ_TILES = 2
PROJ_SPLIT = 4


def _sigmoid(v):
    return 1.0 / (1.0 + jnp.exp(-v))


def _silu(v):
    return v * _sigmoid(v)


def _gelu_tanh(v):
    return 0.5 * v * (1.0 + jnp.tanh(math.sqrt(2.0 / math.pi) * (v + 0.044715 * (v * v * v))))


def _params(*sem):
    return pltpu.CompilerParams(dimension_semantics=sem, vmem_limit_bytes=VMEM_LIMIT)


def _mod_kernel(c_ref, w_ref, b_ref, o_ref):
    s = _silu(c_ref[...])
    o_ref[...] = jnp.dot(s, w_ref[...], preferred_element_type=F32,
                         precision=lax.Precision.HIGHEST) + b_ref[...]


def _modulation(cc, w_mod, b_mod):
    L, D, _ = w_mod.shape
    R = cc.shape[0]
    return pl.pallas_call(
        _mod_kernel,
        grid=(L, N_MOD),
        in_specs=[pl.BlockSpec((R, D), lambda l, j: (0, 0)),
                  pl.BlockSpec((None, D, D), lambda l, j: (l, 0, j)),
                  pl.BlockSpec((None, 1, D), lambda l, j: (l, 0, j))],
        out_specs=pl.BlockSpec((None, R, D), lambda l, j: (l, 0, j)),
        out_shape=jax.ShapeDtypeStruct((L, R, N_MOD * D), F32),
        compiler_params=_params("arbitrary", "arbitrary"),
        name="modulation",
    )(cc, w_mod, b_mod.reshape(L, 1, N_MOD * D))


def _seg_index(n_ctx_tiles):
    return lambda i: jnp.where(i >= n_ctx_tiles, 1, 0)


def _mod_norm(x, g, shift, scale):
    y = x * lax.rsqrt(jnp.mean(x * x, axis=-1, keepdims=True) + NORM_EPS) * g
    return y * (1.0 + scale) + shift


def _rope(t, cos, sin_signed):
    outs = []
    for k in range(t.shape[1] // LANES):
        tk = t[:, k * LANES:(k + 1) * LANES]
        outs.append(tk * cos + pltpu.roll(tk, LANES // 2, 1) * sin_signed)
    return jnp.concatenate(outs, axis=1)


def _proj_kernel(h_ref, g_ref, m_ref, w_ref, b_ref, cos_ref, sin_ref, o_ref, u_scr, *, CL, tiles_per_seq, split):
    i, j = pl.program_id(0), pl.program_id(1)
    tm = h_ref.shape[0]

    rb = tm // split

    def normed(k):
        sl = slice(k * rb, (k + 1) * rb)
        m = m_ref[...]
        row = (i % tiles_per_seq) * tm + k * rb + lax.broadcasted_iota(jnp.int32, (rb, 1), 0)
        is_ctx = row < CL
        shift = jnp.where(is_ctx, m[0, 0:1], m[1, 0:1])
        scale = jnp.where(is_ctx, m[0, 1:2], m[1, 1:2])
        u = _mod_norm(h_ref[sl, :], g_ref[...], shift, scale).astype(u_scr.dtype)
        u_scr[sl, :] = u
        return u

    def project(k, first):
        sl = slice(k * rb, (k + 1) * rb)
        u = normed(k) if first else u_scr[sl, :]
        return sl, jnp.dot(u, w_ref[...], preferred_element_type=F32) + b_ref[...]

    @pl.when(j == 0)
    def _():
        for k in range(split):
            sl, acc = project(k, True)
            o_ref[sl, :] = acc.astype(o_ref.dtype)

    is_rope = jnp.logical_or(j == COL_K, j == COL_Q)

    @pl.when(is_rope)
    def _():
        for k in range(split):
            sl, acc = project(k, False)
            o_ref[sl, :] = _rope(acc, cos_ref[sl, :], sin_ref[sl, :]).astype(o_ref.dtype)

    @pl.when(jnp.logical_and(j != 0, jnp.logical_not(is_rope)))
    def _():
        for k in range(split):
            sl, acc = project(k, False)
            o_ref[sl, :] = acc.astype(o_ref.dtype)


def _in_proj(h2, g, modtab, w, b, cos_t, sin_t, cl, tm, split):
    M, D = h2.shape
    N = w.shape[1]
    T = cos_t.shape[0]
    tiles_per_seq = T // tm
    assert T % tm == 0 and tm % (split * BF16_ROWS) == 0
    return pl.pallas_call(
        functools.partial(_proj_kernel, CL=cl, tiles_per_seq=tiles_per_seq, split=split),
        grid=(M // tm, N // D),
        in_specs=[pl.BlockSpec((tm, D), lambda i, j: (i, 0)),
                  pl.BlockSpec((1, D), lambda i, j: (0, 0)),
                  pl.BlockSpec((None, 2, 8, D), lambda i, j: (i // tiles_per_seq, 0, 0, 0)),
                  pl.BlockSpec((D, D), lambda i, j: (0, j)),
                  pl.BlockSpec((1, D), lambda i, j: (0, j)),
                  pl.BlockSpec((tm, LANES), lambda i, j: (i % tiles_per_seq, 0)),
                  pl.BlockSpec((tm, LANES), lambda i, j: (i % tiles_per_seq, 0))],
        out_specs=pl.BlockSpec((tm, D), lambda i, j: (i, j)),
        out_shape=jax.ShapeDtypeStruct((M, N), BF16),
        scratch_shapes=[pltpu.VMEM((tm, D), BF16)],
        compiler_params=_params("parallel", "arbitrary"),
        name="in_proj",
    )(h2, g.reshape(1, D), modtab, w, b.reshape(1, N), cos_t, sin_t)


def _conv_rows(ext, taps, rows_out):
    n = ext.shape[0]
    rolled = {}
    acc = None
    for off, w in taps:
        start = HALO + off
        r, q = start % SUBLANES, start // SUBLANES
        if r not in rolled:
            rolled[r] = ext if r == 0 else pltpu.roll(ext, n - r, 0)
        term = w * rolled[r][q * SUBLANES:q * SUBLANES + rows_out]
        acc = term if acc is None else acc + term
    return acc


def _rglru_kernel(x_ref, cw_ref, cb_ref, wa_ref, ba_ref, wx_ref, bx_ref, lam_ref, o_ref,
                  xc_scr, h_scr, *, T, CL, TC):
    n_chunks, n_ctx = T // TC, CL // TC
    G = TC // SUBLANES
    cb = x_ref.shape[1]
    cw = cw_ref[...]
    taps = [(k - 2, cw[k:k + 1]) for k in range(RNN_CONV)]
    sub = lax.broadcasted_iota(jnp.int32, (G, SUBLANES, cb), 1)

    def conv_chunk(ci, _):
        t0 = pl.multiple_of(ci * TC, TC)
        seg_first = jnp.logical_or(ci == 0, ci == n_ctx)
        seg_last = jnp.logical_or(ci == n_ctx - 1, ci == n_chunks - 1)
        p0 = pl.multiple_of(jnp.maximum(t0 - HALO, 0), HALO)
        n0 = pl.multiple_of(jnp.minimum(t0 + TC, T - HALO), HALO)
        prev = x_ref[pl.ds(p0, HALO), :].astype(F32) * jnp.where(seg_first, 0.0, 1.0)
        nxt = x_ref[pl.ds(n0, HALO), :].astype(F32) * jnp.where(seg_last, 0.0, 1.0)
        main = x_ref[pl.ds(t0, TC), :].astype(F32)
        xc_scr[pl.ds(t0, TC), :] = _conv_rows(jnp.concatenate([prev, main, nxt], axis=0), taps, TC) + cb_ref[...]
        return 0

    lax.fori_loop(0, n_chunks, conv_chunk, 0)

    def sweep(fwd):
        d = 0 if fwd else 1
        neg_lam = -lam_ref[d]
        softplus = jnp.maximum(neg_lam, 0.0) + jnp.log(1.0 + jnp.exp(-jnp.abs(neg_lam)))

        def chunk(step, carry):
            if fwd:
                ci = step
            else:
                ci = jnp.where(step < n_ctx, n_ctx - 1 - step, n_chunks - 1 - (step - n_ctx))
            t0 = pl.multiple_of(ci * TC, TC)
            xc = xc_scr[pl.ds(t0, TC), :]
            xb = xc.astype(BF16)
            r = _sigmoid(jnp.dot(xb, wa_ref[d], preferred_element_type=F32) + ba_ref[d])
            gi = _sigmoid(jnp.dot(xb, wx_ref[d], preferred_element_type=F32) + bx_ref[d])
            log_a = (-RNN_C) * r * softplus
            a = jnp.exp(log_a)
            gap = 1.0 - a * a
            root = jnp.where(gap > 0.0, gap * lax.rsqrt(gap), 0.0)
            bb = root * (gi * xc)
            a3 = a.reshape(G, SUBLANES, cb)
            b3 = bb.reshape(G, SUBLANES, cb)
            for s in (1, 2, 4):
                shift = s if fwd else SUBLANES - s
                use = (sub >= s) if fwd else (sub < SUBLANES - s)
                a_n, b_n = pltpu.roll(a3, shift, 1), pltpu.roll(b3, shift, 1)
                b3 = jnp.where(use, a3 * b_n + b3, b3)
                a3 = jnp.where(use, a3 * a_n, a3)

            hs = [None] * G
            for k in range(G):
                g = k if fwd else G - 1 - k
                hs[g] = b3[g] + a3[g] * carry
                carry = hs[g][SUBLANES - 1:SUBLANES] if fwd else hs[g][0:1]
            h = jnp.concatenate(hs, axis=0)
            if fwd:
                h_scr[pl.ds(t0, TC), :] = h
            else:
                o_ref[pl.ds(t0, TC), :] = (h_scr[pl.ds(t0, TC), :] + h).astype(o_ref.dtype)
            return carry

        lax.fori_loop(0, n_chunks, chunk, jnp.zeros((1, cb), F32), unroll=True)

    sweep(True)
    sweep(False)


def _rglru(p3, conv_w, conv_b, wa, ba, wx, bx, lam, cl):
    B, T, _ = p3.shape
    D = conv_w.shape[1]
    nb = D // MXU_DIM
    vec = lambda: pl.BlockSpec((2, 1, MXU_DIM), lambda b, c: (0, 0, c))
    mat = lambda: pl.BlockSpec((2, None, MXU_DIM, MXU_DIM), lambda b, c: (0, c, 0, 0))
    return pl.pallas_call(
        functools.partial(_rglru_kernel, T=T, CL=cl, TC=RNN_CHUNK),
        grid=(B, nb),
        in_specs=[pl.BlockSpec((None, T, MXU_DIM), lambda b, c: (b, 0, COL_RX * nb + c)),
                  pl.BlockSpec((RNN_CONV, MXU_DIM), lambda b, c: (0, c)),
                  pl.BlockSpec((1, MXU_DIM), lambda b, c: (0, c)),
                  mat(), vec(), mat(), vec(), vec()],
        out_specs=pl.BlockSpec((None, T, MXU_DIM), lambda b, c: (b, 0, c)),
        out_shape=jax.ShapeDtypeStruct((B, T, D), BF16),
        scratch_shapes=[pltpu.VMEM((T, MXU_DIM), F32), pltpu.VMEM((T, MXU_DIM), F32)],
        compiler_params=_params("parallel", "parallel"),
        name="rglru",
    )(p3, conv_w, conv_b.reshape(1, D), wa, ba.reshape(2, 1, D), wx, bx.reshape(2, 1, D),
      lam.reshape(2, 1, D))


def _attn_kernel(q_ref, k_ref, v_ref, lq_ref, g_ref, o_ref, vt_scr, sa0, sa1, sb0, sb1, *, T, CL, lam_init, do_ctx, kc):
    TQ = ATTN_TQ
    n_lat = (T - CL) // TQ
    for c in range(T // TQ):
        vt_scr[:, c * TQ:(c + 1) * TQ] = v_ref[c * TQ:(c + 1) * TQ, :].T

    lq = lq_ref[...]
    lam = (jnp.exp(jnp.sum(lq[0:1] * lq[1:2], axis=1, keepdims=True))
           - jnp.exp(jnp.sum(lq[2:3] * lq[3:4], axis=1, keepdims=True)) + lam_init)

    def step(key_rows, fin, nxt):
        if nxt is not None:
            r_n, s_n = nxt
            qt = (q_ref[pl.ds(r_n, TQ), :].astype(F32) * (HEAD_DIM ** -0.5 * math.log2(math.e))).T
            row = lax.broadcasted_iota(jnp.int32, qt.shape, 0)
            map0 = (row % HEAD_DIM) < (HEAD_DIM // 2)
            qts = (jnp.where(map0, qt, 0.0).astype(BF16), jnp.where(map0, 0.0, qt).astype(BF16))
            m_n = [jnp.full((1, TQ), -1e30, F32)] * 2
        if fin is not None:
            r_f, s_f, m_f = fin
            l_f = [jnp.zeros((1, TQ), F32)] * 2
            acc = [jnp.zeros((VALUE_DIM, TQ), F32)] * 2
        for k0 in range(0, key_rows, kc):
            size = min(kc, key_rows - k0)
            if fin is not None:
                vt = vt_scr[:, k0:k0 + size]
                for i in range(2):
                    p = jnp.exp2(s_f[i][k0:k0 + size, :] - m_f[i])
                    l_f[i] = l_f[i] + jnp.sum(p, axis=0, keepdims=True)
                    acc[i] = acc[i] + jnp.dot(vt, p.astype(BF16), preferred_element_type=F32)
            if nxt is not None:
                kt = k_ref[k0:k0 + size, :]
                for i in range(2):
                    s = jnp.dot(kt, qts[i], preferred_element_type=F32)
                    s_n[i][k0:k0 + size, :] = s
                    m_n[i] = jnp.maximum(m_n[i], jnp.max(s, axis=0, keepdims=True))
        if fin is not None:
            o = acc[0] / l_f[0] - lam * (acc[1] / l_f[1])
            y = o * lax.rsqrt(jnp.mean(o * o, axis=0, keepdims=True) + NORM_EPS) * g_ref[...]
            o_ref[pl.ds(r_f, TQ), :] = (y * (1.0 - lam_init)).T.astype(o_ref.dtype)
        return tuple(m_n) if nxt is not None else None

    buf_a, buf_b = (sa0, sa1), (sb0, sb1)
    if do_ctx:
        for t in range(CL // TQ):
            m = step(CL, None, (t * TQ, buf_a))
            step(CL, (t * TQ, buf_a, m), None)
    else:
        o_ref[0:CL, :] = jnp.zeros((CL, VALUE_DIM), o_ref.dtype)

    def pair(jj, ma):
        r = pl.multiple_of(CL + 2 * jj * TQ, TQ)
        mb = step(T, (r, buf_a, ma), (r + TQ, buf_b))
        return step(T, (r + TQ, buf_b, mb), (r + 2 * TQ, buf_a))

    ma = lax.fori_loop(0, n_lat // 2 - 1, pair, step(T, None, (CL, buf_a)))
    r = T - 2 * TQ
    mb = step(T, (r, buf_a, ma), (r + TQ, buf_b))
    step(T, (r + TQ, buf_b, mb), None)


def _attention(p3, lq, g_subln, cl, lam_init, do_ctx, kc):
    B, T, _ = p3.shape
    assert kc % LANES == 0 and cl % LANES == 0 and T % LANES == 0
    D = ATTN_HEADS * VALUE_DIM
    cpb = D // VALUE_DIM
    seq = lambda col: pl.BlockSpec((None, T, VALUE_DIM), lambda b, h: (b, 0, col * cpb + h))
    return pl.pallas_call(
        functools.partial(_attn_kernel, T=T, CL=cl, lam_init=lam_init, do_ctx=do_ctx, kc=kc),
        grid=(B, ATTN_HEADS),
        in_specs=[seq(COL_Q), seq(COL_K), seq(COL_V),
                  pl.BlockSpec((4, HEAD_DIM), lambda b, h: (0, 0)),
                  pl.BlockSpec((VALUE_DIM, 1), lambda b, h: (0, 0))],
        out_specs=pl.BlockSpec((None, T, VALUE_DIM), lambda b, h: (b, 0, h)),
        out_shape=jax.ShapeDtypeStruct((B, T, D), BF16),
        scratch_shapes=[pltpu.VMEM((VALUE_DIM, T), BF16)] + [pltpu.VMEM((T, ATTN_TQ), F32)] * 4,
        compiler_params=_params("parallel", "parallel"),
        name="diff_attention",
    )(p3, p3, p3, lq, g_subln.reshape(VALUE_DIM, 1))


def _conv_kernel(v_ref, vp_ref, vn_ref, g_ref, gp_ref, gn_ref, w_ref, b_ref, lg_ref, lb_ref, o_ref,
                 *, T, CL, TC):
    i = pl.program_id(1)
    n_ctx, n_chunks = CL // TC, T // TC
    seg_first = jnp.logical_or(i == 0, i == n_ctx)
    seg_last = jnp.logical_or(i == n_ctx - 1, i == n_chunks - 1)

    def gated(v, g):
        return v[...].astype(F32) * _sigmoid(g[...].astype(F32))

    ext = jnp.concatenate([gated(vp_ref, gp_ref) * jnp.where(seg_first, 0.0, 1.0),
                           gated(v_ref, g_ref),
                           gated(vn_ref, gn_ref) * jnp.where(seg_last, 0.0, 1.0)], axis=0)
    w = w_ref[...]
    cols = []
    for c in range(ext.shape[1] // LANES):
        sl = slice(c * LANES, (c + 1) * LANES)
        taps = [(k - (CONV_K - 1) // 2, w[k:k + 1, sl]) for k in range(CONV_K)]
        cols.append(_conv_rows(ext[:, sl], taps, TC))
    z = jnp.concatenate(cols, axis=1) + b_ref[...]
    mu = jnp.mean(z, axis=-1, keepdims=True)
    zc = z - mu
    var = jnp.mean(zc * zc, axis=-1, keepdims=True)
    y = zc * lax.rsqrt(var + NORM_EPS) * lg_ref[...] + lb_ref[...]
    o_ref[...] = _silu(y).astype(o_ref.dtype)


def _conformer_conv(p3, dw_w, dw_b, ln_g, ln_b, cl):
    B, T, _ = p3.shape
    D = dw_w.shape[1]
    TC = ROW_TILE
    hb = TC // HALO
    last = T // HALO - 1

    def main(col):
        return pl.BlockSpec((None, TC, D), lambda b, i: (b, i, col))

    def prev(col):
        return pl.BlockSpec((None, HALO, D), lambda b, i: (b, jnp.maximum(i * hb - 1, 0), col))

    def nxt(col):
        return pl.BlockSpec((None, HALO, D), lambda b, i: (b, jnp.minimum((i + 1) * hb, last), col))

    vec = lambda: pl.BlockSpec((1, D), lambda b, i: (0, 0))
    return pl.pallas_call(
        functools.partial(_conv_kernel, T=T, CL=cl, TC=TC),
        grid=(B, T // TC),
        in_specs=[main(COL_CV), prev(COL_CV), nxt(COL_CV), main(COL_CG), prev(COL_CG), nxt(COL_CG),
                  pl.BlockSpec((CONV_K, D), lambda b, i: (0, 0)), vec(), vec(), vec()],
        out_specs=pl.BlockSpec((None, TC, D), lambda b, i: (b, i, 0)),
        out_shape=jax.ShapeDtypeStruct((B, T, D), BF16),
        compiler_params=_params("parallel", "parallel"),
        name="conformer_conv",
    )(p3, p3, p3, p3, p3, p3, dw_w, dw_b.reshape(1, D), ln_g.reshape(1, D), ln_b.reshape(1, D))


def _token_kernel(hr_ref, rg_ref, oa_ref, zc_ref, g0_ref, g1_ref, g2_ref, h_ref, m_ref, wr_ref, wa_ref, wc_ref,
                  wo_ref, gn_ref, wi_ref, wf_ref, *rest, d_ff, final):
    f32 = lambda r: r[...].astype(F32)
    m = m_ref[...]
    rec = f32(hr_ref) * _gelu_tanh(f32(rg_ref))
    y_r = jnp.dot(rec.astype(BF16), wr_ref[...], preferred_element_type=F32)
    y_a = jnp.dot(oa_ref[...], wa_ref[...], preferred_element_type=F32)
    y_c = jnp.dot(zc_ref[...], wc_ref[...], preferred_element_type=F32)
    mix = _sigmoid(f32(g0_ref)) * y_r + _sigmoid(f32(g1_ref)) * y_a + _sigmoid(f32(g2_ref)) * y_c
    h = h_ref[...] + m[2:3] * jnp.dot(mix.astype(BF16), wo_ref[...], preferred_element_type=F32)
    u = _mod_norm(h, gn_ref[...], m[3:4], m[4:5]).astype(BF16)
    gu = jnp.dot(u, wi_ref[...], preferred_element_type=F32)
    act = (_silu(gu[:, :d_ff]) * gu[:, d_ff:]).astype(BF16)
    h = h + m[5:6] * jnp.dot(act, wf_ref[...], preferred_element_type=F32)
    if final:
        gf_ref, o_ref = rest
        o_ref[...] = h * lax.rsqrt(jnp.mean(h * h, axis=-1, keepdims=True) + NORM_EPS) * gf_ref[...]
    else:
        (o_ref,) = rest
        o_ref[...] = h


def _token_update(hr, p3, oa, zc, h, modtab, w_r, w_a, w_c, w_o, g2, w_i, w_f, cl, first_tile, g_final=None):
    B, T, D = h.shape
    d_ff = w_f.shape[0]
    seg_of = _seg_index(cl // ROW_TILE)
    seg = lambda i: seg_of(i + first_tile)
    tile = lambda col: pl.BlockSpec((None, ROW_TILE, D), lambda b, i: (b, i + first_tile, col))
    once = lambda shape: pl.BlockSpec(shape, lambda b, i: (0, 0), pipeline_mode=pl.Buffered(1))
    in_specs = [tile(0), tile(COL_RG), tile(0), tile(0), tile(COL_G), tile(COL_G + 1), tile(COL_G + 2), tile(0),
                pl.BlockSpec((None, None, 8, D), lambda b, i: (b, seg(i), 0, 0)),
                once((D, D)), once((D, D)), once((D, D)), once((D, D)),
                once((1, D)), once((D, 2 * d_ff)), once((d_ff, D))]
    args = [hr, p3, oa, zc, p3, p3, p3, h, modtab, w_r, w_a, w_c, w_o, g2.reshape(1, D), w_i, w_f]
    grid = (B, T // ROW_TILE - first_tile)
    if g_final is None:
        return pl.pallas_call(
            functools.partial(_token_kernel, d_ff=d_ff, final=False), grid=grid, in_specs=in_specs,
            out_specs=tile(0), out_shape=jax.ShapeDtypeStruct((B, T, D), F32), input_output_aliases={7: 0},
            compiler_params=_params("parallel", "parallel"), name="token_update",
        )(*args)
    assert first_tile * ROW_TILE == cl
    return pl.pallas_call(
        functools.partial(_token_kernel, d_ff=d_ff, final=True), grid=grid, in_specs=in_specs + [once((1, D))],
        out_specs=pl.BlockSpec((None, ROW_TILE, D), lambda b, i: (b, i, 0)),
        out_shape=jax.ShapeDtypeStruct((B, T - cl, D), F32),
        compiler_params=_params("parallel", "parallel"), name="token_update_final",
    )(*args, g_final.reshape(1, D))


def _rope_tables(cl, s):
    pairs_axis = HEAD_DIM // 4
    rows = jnp.repeat(jnp.arange(s // GRID_W, dtype=F32), GRID_W)
    cols = jnp.tile(jnp.arange(GRID_W, dtype=F32), s // GRID_W)
    inv = ROPE_BASE ** (-jnp.arange(pairs_axis, dtype=F32) / pairs_axis)
    ang = jnp.concatenate([rows[:, None] * inv, cols[:, None] * inv], axis=-1)
    cos = jnp.concatenate([jnp.ones((cl, HEAD_DIM // 2), F32), jnp.cos(ang)], axis=0)
    sin = jnp.concatenate([jnp.zeros((cl, HEAD_DIM // 2), F32), jnp.sin(ang)], axis=0)
    return (jnp.concatenate([cos, cos, cos, cos], axis=1),
            jnp.concatenate([-sin, -sin, sin, sin], axis=1))


def _rope_layout(a):
    lead = a.shape[:-1]
    a = a.reshape(lead + (ATTN_HEADS, 2, 2, HEAD_DIM // 2))
    return jnp.swapaxes(a, -3, -2).reshape(lead + (ATTN_HEADS * VALUE_DIM,))


def _permute_qk_columns(a):
    d = a.shape[-1] // N_COL_BLOCKS
    blocks = [a[..., i * d:(i + 1) * d] for i in range(N_COL_BLOCKS)]
    for i in (COL_K, COL_Q):
        blocks[i] = _rope_layout(blocks[i])
    return jnp.concatenate(blocks, axis=-1)


def _blockdiag_tiles(w):
    two, nb, bs, _ = w.shape
    per = MXU_DIM // bs
    w = w.reshape(two, nb // per, per, bs, bs)
    eye = jnp.eye(per, dtype=w.dtype)
    t = jnp.einsum('dtpio,pq->dtpiqo', w, eye)
    return t.reshape(two, nb // per, MXU_DIM, MXU_DIM)


def kernel(x, c, ctx, c_ctx, w_mod, b_mod, g_norm1, g_norm2, w_in, b_in, rnn_conv_w, rnn_conv_b, rnn_w_a, rnn_b_a, rnn_w_x, rnn_b_x, rnn_lambda, w_rnn_o, lambda_qk, g_subln, w_attn_o, conv_dw_w, conv_dw_b, conv_ln_g, conv_ln_b, w_conv_o, w_out, w_ffn_in, w_ffn_out, g_final):
    B, S, D = x.shape
    CL = ctx.shape[1]
    T = CL + S
    L = w_mod.shape[0]
    assert D == ATTN_HEADS * VALUE_DIM and S % GRID_W == 0
    assert CL % ROW_TILE == 0 and S % ROW_TILE == 0 and ROW_TILE == ATTN_TQ
    assert (S // ATTN_TQ) % 2 == 0

    h = jnp.concatenate([ctx, x], axis=1)

    n_cond = -(-(B + 1) // SUBLANES) * SUBLANES
    cc = jnp.zeros((n_cond, D), F32).at[:B].set(c).at[B].set(c_ctx)
    mod = _modulation(cc, w_mod, b_mod).reshape(L, n_cond, N_MOD, D)
    mod_ctx = jnp.broadcast_to(mod[:, B][:, None], (L, B, N_MOD, D))
    modtab = jnp.stack([mod_ctx, mod[:, :B]], axis=2)
    modtab = jnp.pad(modtab, ((0, 0), (0, 0), (0, 0), (0, 8 - N_MOD), (0, 0)))

    cos_t, sin_t = _rope_tables(CL, S)
    w_in_b = _permute_qk_columns(w_in).astype(BF16)
    b_in_p = _permute_qk_columns(b_in)

    for l in range(L):
        lam_init = 0.8 - 0.6 * math.exp(-0.3 * l)
        first_tile = CL // ROW_TILE if l == L - 1 else 0
        p = _in_proj(h.reshape(B * T, D), g_norm1[l], modtab[l], w_in_b[l], b_in_p[l], cos_t, sin_t, CL,
                     T // PROJ_TILES, PROJ_SPLIT)
        p3 = p.reshape(B, T, N_COL_BLOCKS * D)
        hr = _rglru(p3, rnn_conv_w[l], rnn_conv_b[l],
                    _blockdiag_tiles(rnn_w_a[l]).astype(BF16), rnn_b_a[l],
                    _blockdiag_tiles(rnn_w_x[l]).astype(BF16), rnn_b_x[l], rnn_lambda[l], CL)
        oa = _attention(p3, lambda_qk[l], g_subln[l], CL, lam_init, first_tile == 0, ATTN_KC)
        zc = _conformer_conv(p3, conv_dw_w[l], conv_dw_b[l], conv_ln_g[l], conv_ln_b[l], CL)
        h = _token_update(hr, p3, oa, zc, h, modtab[l], w_rnn_o[l].astype(BF16), w_attn_o[l].astype(BF16),
                          w_conv_o[l].astype(BF16), w_out[l].astype(BF16), g_norm2[l], w_ffn_in[l].astype(BF16),
                          w_ffn_out[l].astype(BF16), CL, first_tile, g_final if l == L - 1 else None)
    return h
```

```python
import functools
import math

import jax
import jax.numpy as jnp
from jax import lax
from jax.experimental import pallas as pl
from jax.experimental.pallas import tpu as pltpu

F32 = jnp.float32
BF16 = jnp.bfloat16

NORM_EPS = 1e-6
N_MOD = 6
ATTN_HEADS = 8
HEAD_DIM = 64
VALUE_DIM = 2 * HEAD_DIM
GRID_W = 64
ROPE_BASE = 10000.0
RNN_BLOCK = 64
RNN_CONV = 4
RNN_C = 8.0
CONV_K = 31
LANES = 128
SUBLANES = 8
BF16_ROWS = 16
HALO = 16
MXU_DIM = 256
VMEM_LIMIT = 56 * 1024 * 1024

COL_RX, COL_K, COL_V, COL_RG, COL_Q, COL_CV, COL_CG, COL_G = 0, 1, 2, 3, 4, 5, 6, 7
N_COL_BLOCKS = 10

ROW_TILE = 256
RNN_CHUNK = 256
ATTN_TQ = 256
ATTN_KC = 256
PROJ_TILES = 2
PROJ_SPLIT = 4


def _sigmoid(v):
    return 1.0 / (1.0 + jnp.exp(-v))


def _silu(v):
    return v * _sigmoid(v)


def _gelu_tanh(v):
    return 0.5 * v * (1.0 + jnp.tanh(math.sqrt(2.0 / math.pi) * (v + 0.044715 * (v * v * v))))


def _params(*sem):
    return pltpu.CompilerParams(dimension_semantics=sem, vmem_limit_bytes=VMEM_LIMIT)


def _mod_kernel(c_ref, w_ref, b_ref, o_ref):
    s = _silu(c_ref[...])
    o_ref[...] = jnp.dot(s, w_ref[...], preferred_element_type=F32,
                         precision=lax.Precision.HIGHEST) + b_ref[...]


def _modulation(cc, w_mod, b_mod):
    L, D, _ = w_mod.shape
    R = cc.shape[0]
    return pl.pallas_call(
        _mod_kernel,
        grid=(L, N_MOD),
        in_specs=[pl.BlockSpec((R, D), lambda l, j: (0, 0)),
                  pl.BlockSpec((None, D, D), lambda l, j: (l, 0, j)),
                  pl.BlockSpec((None, 1, D), lambda l, j: (l, 0, j))],
        out_specs=pl.BlockSpec((None, R, D), lambda l, j: (l, 0, j)),
        out_shape=jax.ShapeDtypeStruct((L, R, N_MOD * D), F32),
        compiler_params=_params("arbitrary", "arbitrary"),
        name="modulation",
    )(cc, w_mod, b_mod.reshape(L, 1, N_MOD * D))


def _seg_index(n_ctx_tiles):
    return lambda i: jnp.where(i >= n_ctx_tiles, 1, 0)


def _mod_norm(x, g, shift, scale):
    y = x * lax.rsqrt(jnp.mean(x * x, axis=-1, keepdims=True) + NORM_EPS) * g
    return y * (1.0 + scale) + shift


def _rope(t, cos, sin_signed):
    outs = []
    for k in range(t.shape[1] // LANES):
        tk = t[:, k * LANES:(k + 1) * LANES]
        outs.append(tk * cos + pltpu.roll(tk, LANES // 2, 1) * sin_signed)
    return jnp.concatenate(outs, axis=1)


def _proj_kernel(h_ref, g_ref, m_ref, w_ref, b_ref, cos_ref, sin_ref, o_ref, u_scr, *, CL, tiles_per_seq, split):
    i, j = pl.program_id(0), pl.program_id(1)
    tm = h_ref.shape[0]

    rb = tm // split

    def normed(k):
        sl = slice(k * rb, (k + 1) * rb)
        m = m_ref[...]
        row = (i % tiles_per_seq) * tm + k * rb + lax.broadcasted_iota(jnp.int32, (rb, 1), 0)
        is_ctx = row < CL
        shift = jnp.where(is_ctx, m[0, 0:1], m[1, 0:1])
        scale = jnp.where(is_ctx, m[0, 1:2], m[1, 1:2])
        u = _mod_norm(h_ref[sl, :], g_ref[...], shift, scale).astype(u_scr.dtype)
        u_scr[sl, :] = u
        return u

    def project(k, first):
        sl = slice(k * rb, (k + 1) * rb)
        u = normed(k) if first else u_scr[sl, :]
        return sl, jnp.dot(u, w_ref[...], preferred_element_type=F32) + b_ref[...]

    @pl.when(j == 0)
    def _():
        for k in range(split):
            sl, acc = project(k, True)
            o_ref[sl, :] = acc.astype(o_ref.dtype)

    is_rope = jnp.logical_or(j == COL_K, j == COL_Q)

    @pl.when(is_rope)
    def _():
        for k in range(split):
            sl, acc = project(k, False)
            o_ref[sl, :] = _rope(acc, cos_ref[sl, :], sin_ref[sl, :]).astype(o_ref.dtype)

    @pl.when(jnp.logical_and(j != 0, jnp.logical_not(is_rope)))
    def _():
        for k in range(split):
            sl, acc = project(k, False)
            o_ref[sl, :] = acc.astype(o_ref.dtype)


def _in_proj(h2, g, modtab, w, b, layer, cos_t, sin_t, cl, tm, split):
    M, D = h2.shape
    N = w.shape[2]
    T = cos_t.shape[0]
    tiles_per_seq = T // tm
    assert T % tm == 0 and tm % (split * BF16_ROWS) == 0
    return pl.pallas_call(
        functools.partial(_proj_kernel, CL=cl, tiles_per_seq=tiles_per_seq, split=split),
        grid=(M // tm, N // D),
        in_specs=[pl.BlockSpec((tm, D), lambda i, j: (i, 0)),
                  pl.BlockSpec((1, D), lambda i, j: (0, 0)),
                  pl.BlockSpec((None, 2, 8, D), lambda i, j: (i // tiles_per_seq, 0, 0, 0)),
                  pl.BlockSpec((None, D, D), lambda i, j: (layer, 0, j)),
                  pl.BlockSpec((None, 1, D), lambda i, j: (layer, 0, j)),
                  pl.BlockSpec((tm, LANES), lambda i, j: (i % tiles_per_seq, 0)),
                  pl.BlockSpec((tm, LANES), lambda i, j: (i % tiles_per_seq, 0))],
        out_specs=pl.BlockSpec((tm, D), lambda i, j: (i, j)),
        out_shape=jax.ShapeDtypeStruct((M, N), BF16),
        scratch_shapes=[pltpu.VMEM((tm, D), BF16)],
        compiler_params=_params("parallel", "arbitrary"),
        name="in_proj",
    )(h2, g.reshape(1, D), modtab, w, b.reshape(-1, 1, N), cos_t, sin_t)


def _conv_rows(ext, taps, rows_out):
    n = ext.shape[0]
    rolled = {}
    acc = None
    for off, w in taps:
        start = HALO + off
        r, q = start % SUBLANES, start // SUBLANES
        if r not in rolled:
            rolled[r] = ext if r == 0 else pltpu.roll(ext, n - r, 0)
        term = w * rolled[r][q * SUBLANES:q * SUBLANES + rows_out]
        acc = term if acc is None else acc + term
    return acc


def _rglru_kernel(x_ref, cw_ref, cb_ref, wa_ref, ba_ref, wx_ref, bx_ref, lam_ref, o_ref,
                  xc_scr, h_scr, *, T, CL, TC):
    n_chunks, n_ctx = T // TC, CL // TC
    G = TC // SUBLANES
    cb = x_ref.shape[1]
    cw = cw_ref[...]
    taps = [(k - 2, cw[k:k + 1]) for k in range(RNN_CONV)]
    sub = lax.broadcasted_iota(jnp.int32, (G, SUBLANES, cb), 1)

    def conv_chunk(ci, _):
        t0 = pl.multiple_of(ci * TC, TC)
        seg_first = jnp.logical_or(ci == 0, ci == n_ctx)
        seg_last = jnp.logical_or(ci == n_ctx - 1, ci == n_chunks - 1)
        p0 = pl.multiple_of(jnp.maximum(t0 - HALO, 0), HALO)
        n0 = pl.multiple_of(jnp.minimum(t0 + TC, T - HALO), HALO)
        prev = x_ref[pl.ds(p0, HALO), :].astype(F32) * jnp.where(seg_first, 0.0, 1.0)
        nxt = x_ref[pl.ds(n0, HALO), :].astype(F32) * jnp.where(seg_last, 0.0, 1.0)
        main = x_ref[pl.ds(t0, TC), :].astype(F32)
        xc_scr[pl.ds(t0, TC), :] = _conv_rows(jnp.concatenate([prev, main, nxt], axis=0), taps, TC) + cb_ref[...]
        return 0

    lax.fori_loop(0, n_chunks, conv_chunk, 0)

    def sweep(fwd):
        d = 0 if fwd else 1
        neg_lam = -lam_ref[d]
        softplus = jnp.maximum(neg_lam, 0.0) + jnp.log(1.0 + jnp.exp(-jnp.abs(neg_lam)))

        def chunk(step, carry):
            if fwd:
                ci = step
            else:
                ci = jnp.where(step < n_ctx, n_ctx - 1 - step, n_chunks - 1 - (step - n_ctx))
            t0 = pl.multiple_of(ci * TC, TC)
            xc = xc_scr[pl.ds(t0, TC), :]
            xb = xc.astype(BF16)
            r = _sigmoid(jnp.dot(xb, wa_ref[d], preferred_element_type=F32) + ba_ref[d])
            gi = _sigmoid(jnp.dot(xb, wx_ref[d], preferred_element_type=F32) + bx_ref[d])
            log_a = (-RNN_C) * r * softplus
            a = jnp.exp(log_a)
            gap = 1.0 - a * a
            root = jnp.where(gap > 0.0, gap * lax.rsqrt(gap), 0.0)
            bb = root * (gi * xc)
            a3 = a.reshape(G, SUBLANES, cb)
            b3 = bb.reshape(G, SUBLANES, cb)
            for s in (1, 2, 4):
                shift = s if fwd else SUBLANES - s
                use = (sub >= s) if fwd else (sub < SUBLANES - s)
                a_n, b_n = pltpu.roll(a3, shift, 1), pltpu.roll(b3, shift, 1)
                b3 = jnp.where(use, a3 * b_n + b3, b3)
                a3 = jnp.where(use, a3 * a_n, a3)

            hs = [None] * G
            for k in range(G):
                g = k if fwd else G - 1 - k
                hs[g] = b3[g] + a3[g] * carry
                carry = hs[g][SUBLANES - 1:SUBLANES] if fwd else hs[g][0:1]
            h = jnp.concatenate(hs, axis=0)
            if fwd:
                h_scr[pl.ds(t0, TC), :] = h
            else:
                o_ref[pl.ds(t0, TC), :] = (h_scr[pl.ds(t0, TC), :] + h).astype(o_ref.dtype)
            return carry

        lax.fori_loop(0, n_chunks, chunk, jnp.zeros((1, cb), F32), unroll=True)

    sweep(True)
    sweep(False)


def _rglru(p3, conv_w, conv_b, wa, ba, wx, bx, lam, cl):
    B, T, _ = p3.shape
    D = conv_w.shape[1]
    nb = D // MXU_DIM
    vec = lambda: pl.BlockSpec((2, 1, MXU_DIM), lambda b, c: (0, 0, c))
    mat = lambda: pl.BlockSpec((2, None, MXU_DIM, MXU_DIM), lambda b, c: (0, c, 0, 0))
    return pl.pallas_call(
        functools.partial(_rglru_kernel, T=T, CL=cl, TC=RNN_CHUNK),
        grid=(B, nb),
        in_specs=[pl.BlockSpec((None, T, MXU_DIM), lambda b, c: (b, 0, COL_RX * nb + c)),
                  pl.BlockSpec((RNN_CONV, MXU_DIM), lambda b, c: (0, c)),
                  pl.BlockSpec((1, MXU_DIM), lambda b, c: (0, c)),
                  mat(), vec(), mat(), vec(), vec()],
        out_specs=pl.BlockSpec((None, T, MXU_DIM), lambda b, c: (b, 0, c)),
        out_shape=jax.ShapeDtypeStruct((B, T, D), BF16),
        scratch_shapes=[pltpu.VMEM((T, MXU_DIM), F32), pltpu.VMEM((T, MXU_DIM), F32)],
        compiler_params=_params("parallel", "parallel"),
        name="rglru",
    )(p3, conv_w, conv_b.reshape(1, D), wa, ba.reshape(2, 1, D), wx, bx.reshape(2, 1, D),
      lam.reshape(2, 1, D))


def _attn_kernel(q_ref, k_ref, v_ref, lq_ref, g_ref, o_ref, vt_scr, sa0, sa1, sb0, sb1, *, T, CL, lam_init, do_ctx, kc):
    TQ = ATTN_TQ
    n_lat = (T - CL) // TQ
    for c in range(T // TQ):
        vt_scr[:, c * TQ:(c + 1) * TQ] = v_ref[c * TQ:(c + 1) * TQ, :].T

    lq = lq_ref[...]
    lam = (jnp.exp(jnp.sum(lq[0:1] * lq[1:2], axis=1, keepdims=True))
           - jnp.exp(jnp.sum(lq[2:3] * lq[3:4], axis=1, keepdims=True)) + lam_init)

    def step(key_rows, fin, nxt, fin_rows=None):
        if nxt is not None:
            r_n, s_n = nxt
            qt = (q_ref[pl.ds(r_n, TQ), :].astype(F32) * (HEAD_DIM ** -0.5 * math.log2(math.e))).T
            row = lax.broadcasted_iota(jnp.int32, qt.shape, 0)
            map0 = (row % HEAD_DIM) < (HEAD_DIM // 2)
            qts = (jnp.where(map0, qt, 0.0).astype(BF16), jnp.where(map0, 0.0, qt).astype(BF16))
            m_n = [jnp.full((1, TQ), -1e30, F32)] * 2
        if fin is not None:
            r_f, s_f, m_f = fin
            l_f = [jnp.zeros((1, TQ), F32)] * 2
            acc = [jnp.zeros((VALUE_DIM, TQ), F32)] * 2
        fin_rows = key_rows if fin_rows is None else fin_rows
        for k0 in range(0, key_rows, kc):
            size = min(kc, key_rows - k0)
            if fin is not None and k0 < fin_rows:
                vt = vt_scr[:, k0:k0 + size]
                for i in range(2):
                    p = jnp.exp2(s_f[i][k0:k0 + size, :] - m_f[i])
                    l_f[i] = l_f[i] + jnp.sum(p, axis=0, keepdims=True)
                    acc[i] = acc[i] + jnp.dot(vt, p.astype(BF16), preferred_element_type=F32)
            if nxt is not None:
                kt = k_ref[k0:k0 + size, :]
                for i in range(2):
                    s = jnp.dot(kt, qts[i], preferred_element_type=F32)
                    s_n[i][k0:k0 + size, :] = s
                    m_n[i] = jnp.maximum(m_n[i], jnp.max(s, axis=0, keepdims=True))
        if fin is not None:
            o = acc[0] / l_f[0] - lam * (acc[1] / l_f[1])
            y = o * lax.rsqrt(jnp.mean(o * o, axis=0, keepdims=True) + NORM_EPS) * g_ref[...]
            o_ref[pl.ds(r_f, TQ), :] = (y * (1.0 - lam_init)).T.astype(o_ref.dtype)
        return tuple(m_n) if nxt is not None else None

    buf_a, buf_b = (sa0, sa1), (sb0, sb1)
    if do_ctx:
        last = CL - TQ
        for r in range(0, last, TQ):
            m = step(CL, None, (r, buf_b))
            step(CL, (r, buf_b, m), None)
        m = step(CL, None, (last, buf_b))
        first = step(T, (last, buf_b, m), (CL, buf_a), fin_rows=CL)
    else:
        o_ref[0:CL, :] = jnp.zeros((CL, VALUE_DIM), o_ref.dtype)
        first = step(T, None, (CL, buf_a))

    def pair(jj, ma):
        r = pl.multiple_of(CL + 2 * jj * TQ, TQ)
        mb = step(T, (r, buf_a, ma), (r + TQ, buf_b))
        return step(T, (r + TQ, buf_b, mb), (r + 2 * TQ, buf_a))

    ma = lax.fori_loop(0, n_lat // 2 - 1, pair, first)
    r = T - 2 * TQ
    mb = step(T, (r, buf_a, ma), (r + TQ, buf_b))
    step(T, (r + TQ, buf_b, mb), None)


def _attention(p3, lq, g_subln, cl, lam_init, do_ctx, kc):
    B, T, _ = p3.shape
    assert kc % LANES == 0 and cl % LANES == 0 and T % LANES == 0
    D = ATTN_HEADS * VALUE_DIM
    cpb = D // VALUE_DIM
    seq = lambda col: pl.BlockSpec((None, T, VALUE_DIM), lambda b, h: (b, 0, col * cpb + h))
    return pl.pallas_call(
        functools.partial(_attn_kernel, T=T, CL=cl, lam_init=lam_init, do_ctx=do_ctx, kc=kc),
        grid=(B, ATTN_HEADS),
        in_specs=[seq(COL_Q), seq(COL_K), seq(COL_V),
                  pl.BlockSpec((4, HEAD_DIM), lambda b, h: (0, 0)),
                  pl.BlockSpec((VALUE_DIM, 1), lambda b, h: (0, 0))],
        out_specs=pl.BlockSpec((None, T, VALUE_DIM), lambda b, h: (b, 0, h)),
        out_shape=jax.ShapeDtypeStruct((B, T, D), BF16),
        scratch_shapes=[pltpu.VMEM((VALUE_DIM, T), BF16)] + [pltpu.VMEM((T, ATTN_TQ), F32)] * 4,
        compiler_params=_params("parallel", "parallel"),
        name="diff_attention",
    )(p3, p3, p3, lq, g_subln.reshape(VALUE_DIM, 1))


def _conv_kernel(v_ref, vp_ref, vn_ref, g_ref, gp_ref, gn_ref, w_ref, b_ref, lg_ref, lb_ref, o_ref,
                 *, T, CL, TC):
    i = pl.program_id(1)
    n_ctx, n_chunks = CL // TC, T // TC
    seg_first = jnp.logical_or(i == 0, i == n_ctx)
    seg_last = jnp.logical_or(i == n_ctx - 1, i == n_chunks - 1)

    def gated(v, g):
        return v[...].astype(F32) * _sigmoid(g[...].astype(F32))

    ext = jnp.concatenate([gated(vp_ref, gp_ref) * jnp.where(seg_first, 0.0, 1.0),
                           gated(v_ref, g_ref),
                           gated(vn_ref, gn_ref) * jnp.where(seg_last, 0.0, 1.0)], axis=0)
    w = w_ref[...]
    cols = []
    for c in range(ext.shape[1] // LANES):
        sl = slice(c * LANES, (c + 1) * LANES)
        taps = [(k - (CONV_K - 1) // 2, w[k:k + 1, sl]) for k in range(CONV_K)]
        cols.append(_conv_rows(ext[:, sl], taps, TC))
    z = jnp.concatenate(cols, axis=1) + b_ref[...]
    mu = jnp.mean(z, axis=-1, keepdims=True)
    zc = z - mu
    var = jnp.mean(zc * zc, axis=-1, keepdims=True)
    y = zc * lax.rsqrt(var + NORM_EPS) * lg_ref[...] + lb_ref[...]
    o_ref[...] = _silu(y).astype(o_ref.dtype)


def _conformer_conv(p3, dw_w, dw_b, ln_g, ln_b, cl):
    B, T, _ = p3.shape
    D = dw_w.shape[1]
    TC = ROW_TILE
    hb = TC // HALO
    last = T // HALO - 1

    def main(col):
        return pl.BlockSpec((None, TC, D), lambda b, i: (b, i, col))

    def prev(col):
        return pl.BlockSpec((None, HALO, D), lambda b, i: (b, jnp.maximum(i * hb - 1, 0), col))

    def nxt(col):
        return pl.BlockSpec((None, HALO, D), lambda b, i: (b, jnp.minimum((i + 1) * hb, last), col))

    vec = lambda: pl.BlockSpec((1, D), lambda b, i: (0, 0))
    return pl.pallas_call(
        functools.partial(_conv_kernel, T=T, CL=cl, TC=TC),
        grid=(B, T // TC),
        in_specs=[main(COL_CV), prev(COL_CV), nxt(COL_CV), main(COL_CG), prev(COL_CG), nxt(COL_CG),
                  pl.BlockSpec((CONV_K, D), lambda b, i: (0, 0)), vec(), vec(), vec()],
        out_specs=pl.BlockSpec((None, TC, D), lambda b, i: (b, i, 0)),
        out_shape=jax.ShapeDtypeStruct((B, T, D), BF16),
        compiler_params=_params("parallel", "parallel"),
        name="conformer_conv",
    )(p3, p3, p3, p3, p3, p3, dw_w, dw_b.reshape(1, D), ln_g.reshape(1, D), ln_b.reshape(1, D))


def _token_kernel(hr_ref, rg_ref, oa_ref, zc_ref, g0_ref, g1_ref, g2_ref, h_ref, m_ref, wr_ref, wa_ref, wc_ref,
                  wo_ref, gn_ref, wi_ref, wf_ref, *rest, d_ff, final):
    f32 = lambda r: r[...].astype(F32)
    m = m_ref[...]
    rec = f32(hr_ref) * _gelu_tanh(f32(rg_ref))
    y_r = jnp.dot(rec.astype(BF16), wr_ref[...], preferred_element_type=F32)
    y_a = jnp.dot(oa_ref[...], wa_ref[...], preferred_element_type=F32)
    y_c = jnp.dot(zc_ref[...], wc_ref[...], preferred_element_type=F32)
    mix = _sigmoid(f32(g0_ref)) * y_r + _sigmoid(f32(g1_ref)) * y_a + _sigmoid(f32(g2_ref)) * y_c
    h = h_ref[...] + m[2:3] * jnp.dot(mix.astype(BF16), wo_ref[...], preferred_element_type=F32)
    u = _mod_norm(h, gn_ref[...], m[3:4], m[4:5]).astype(BF16)
    gu = jnp.dot(u, wi_ref[...], preferred_element_type=F32)
    act = (_silu(gu[:, :d_ff]) * gu[:, d_ff:]).astype(BF16)
    h = h + m[5:6] * jnp.dot(act, wf_ref[...], preferred_element_type=F32)
    if final:
        gf_ref, o_ref = rest
        o_ref[...] = h * lax.rsqrt(jnp.mean(h * h, axis=-1, keepdims=True) + NORM_EPS) * gf_ref[...]
    else:
        (o_ref,) = rest
        o_ref[...] = h


def _token_update(hr, p3, oa, zc, h, modtab, w_r, w_a, w_c, w_o, g2, w_i, w_f, layer, cl, first_tile, g_final=None):
    B, T, D = h.shape
    d_ff = w_f.shape[1]
    seg_of = _seg_index(cl // ROW_TILE)
    seg = lambda i: seg_of(i + first_tile)
    tile = lambda col: pl.BlockSpec((None, ROW_TILE, D), lambda b, i: (b, i + first_tile, col))
    once = lambda shape: pl.BlockSpec(shape, lambda b, i: (0, 0), pipeline_mode=pl.Buffered(1))
    of_layer = lambda rows, cols: pl.BlockSpec((None, rows, cols), lambda b, i: (layer, 0, 0),
                                               pipeline_mode=pl.Buffered(1))
    in_specs = [tile(0), tile(COL_RG), tile(0), tile(0), tile(COL_G), tile(COL_G + 1), tile(COL_G + 2), tile(0),
                pl.BlockSpec((None, None, 8, D), lambda b, i: (b, seg(i), 0, 0)),
                of_layer(D, D), of_layer(D, D), of_layer(D, D), of_layer(D, D),
                once((1, D)), of_layer(D, 2 * d_ff), of_layer(d_ff, D)]
    args = [hr, p3, oa, zc, p3, p3, p3, h, modtab, w_r, w_a, w_c, w_o, g2.reshape(1, D), w_i, w_f]
    grid = (B, T // ROW_TILE - first_tile)
    if g_final is None:
        return pl.pallas_call(
            functools.partial(_token_kernel, d_ff=d_ff, final=False), grid=grid, in_specs=in_specs,
            out_specs=tile(0), out_shape=jax.ShapeDtypeStruct((B, T, D), F32), input_output_aliases={7: 0},
            compiler_params=_params("parallel", "parallel"), name="token_update",
        )(*args)
    assert first_tile * ROW_TILE == cl
    return pl.pallas_call(
        functools.partial(_token_kernel, d_ff=d_ff, final=True), grid=grid, in_specs=in_specs + [once((1, D))],
        out_specs=pl.BlockSpec((None, ROW_TILE, D), lambda b, i: (b, i, 0)),
        out_shape=jax.ShapeDtypeStruct((B, T - cl, D), F32),
        compiler_params=_params("parallel", "parallel"), name="token_update_final",
    )(*args, g_final.reshape(1, D))


def _rope_tables(cl, s):
    pairs_axis = HEAD_DIM // 4
    rows = jnp.repeat(jnp.arange(s // GRID_W, dtype=F32), GRID_W)
    cols = jnp.tile(jnp.arange(GRID_W, dtype=F32), s // GRID_W)
    inv = ROPE_BASE ** (-jnp.arange(pairs_axis, dtype=F32) / pairs_axis)
    ang = jnp.concatenate([rows[:, None] * inv, cols[:, None] * inv], axis=-1)
    cos = jnp.concatenate([jnp.ones((cl, HEAD_DIM // 2), F32), jnp.cos(ang)], axis=0)
    sin = jnp.concatenate([jnp.zeros((cl, HEAD_DIM // 2), F32), jnp.sin(ang)], axis=0)
    return (jnp.concatenate([cos, cos, cos, cos], axis=1),
            jnp.concatenate([-sin, -sin, sin, sin], axis=1))


def _rope_layout(a):
    lead = a.shape[:-1]
    a = a.reshape(lead + (ATTN_HEADS, 2, 2, HEAD_DIM // 2))
    return jnp.swapaxes(a, -3, -2).reshape(lead + (ATTN_HEADS * VALUE_DIM,))


def _permute_qk_columns(a):
    d = a.shape[-1] // N_COL_BLOCKS
    blocks = [a[..., i * d:(i + 1) * d] for i in range(N_COL_BLOCKS)]
    for i in (COL_K, COL_Q):
        blocks[i] = _rope_layout(blocks[i])
    return jnp.concatenate(blocks, axis=-1)


def _blockdiag_tiles(w):
    two, nb, bs, _ = w.shape
    per = MXU_DIM // bs
    w = w.reshape(two, nb // per, per, bs, bs)
    eye = jnp.eye(per, dtype=w.dtype)
    t = jnp.einsum('dtpio,pq->dtpiqo', w, eye)
    return t.reshape(two, nb // per, MXU_DIM, MXU_DIM)


def kernel(x, c, ctx, c_ctx, w_mod, b_mod, g_norm1, g_norm2, w_in, b_in, rnn_conv_w, rnn_conv_b, rnn_w_a, rnn_b_a, rnn_w_x, rnn_b_x, rnn_lambda, w_rnn_o, lambda_qk, g_subln, w_attn_o, conv_dw_w, conv_dw_b, conv_ln_g, conv_ln_b, w_conv_o, w_out, w_ffn_in, w_ffn_out, g_final):
    B, S, D = x.shape
    CL = ctx.shape[1]
    T = CL + S
    L = w_mod.shape[0]
    assert D == ATTN_HEADS * VALUE_DIM and S % GRID_W == 0
    assert CL % ROW_TILE == 0 and S % ROW_TILE == 0 and ROW_TILE == ATTN_TQ
    assert (S // ATTN_TQ) % 2 == 0

    h = jnp.concatenate([ctx, x], axis=1)

    n_cond = -(-(B + 1) // SUBLANES) * SUBLANES
    cc = jnp.zeros((n_cond, D), F32).at[:B].set(c).at[B].set(c_ctx)
    mod = _modulation(cc, w_mod, b_mod).reshape(L, n_cond, N_MOD, D)
    mod_ctx = jnp.broadcast_to(mod[:, B][:, None], (L, B, N_MOD, D))
    modtab = jnp.stack([mod_ctx, mod[:, :B]], axis=2)
    modtab = jnp.pad(modtab, ((0, 0), (0, 0), (0, 0), (0, 8 - N_MOD), (0, 0)))

    cos_t, sin_t = _rope_tables(CL, S)
    w_in_b = _permute_qk_columns(w_in).astype(BF16)
    b_in_p = _permute_qk_columns(b_in)
    out_weights = [w.astype(BF16) for w in (w_rnn_o, w_attn_o, w_conv_o, w_out, w_ffn_in, w_ffn_out)]

    for l in range(L):
        lam_init = 0.8 - 0.6 * math.exp(-0.3 * l)
        first_tile = CL // ROW_TILE if l == L - 1 else 0
        p = _in_proj(h.reshape(B * T, D), g_norm1[l], modtab[l], w_in_b, b_in_p, l, cos_t, sin_t, CL,
                     T // PROJ_TILES, PROJ_SPLIT)
        p3 = p.reshape(B, T, N_COL_BLOCKS * D)
        hr = _rglru(p3, rnn_conv_w[l], rnn_conv_b[l],
                    _blockdiag_tiles(rnn_w_a[l]).astype(BF16), rnn_b_a[l],
                    _blockdiag_tiles(rnn_w_x[l]).astype(BF16), rnn_b_x[l], rnn_lambda[l], CL)
        oa = _attention(p3, lambda_qk[l], g_subln[l], CL, lam_init, first_tile == 0, ATTN_KC)
        zc = _conformer_conv(p3, conv_dw_w[l], conv_dw_b[l], conv_ln_g[l], conv_ln_b[l], CL)
        h = _token_update(hr, p3, oa, zc, h, modtab[l], *out_weights[:4], g_norm2[l], *out_weights[4:], l,
                          CL, first_tile, g_final if l == L - 1 else None)
    return h
```
